```python
import math
import jax, jax.numpy as jnp
from jax import lax
import numpy as np

D_MODEL = 1024
BATCH = 2
SEQ = 8192
DEPTH = 4
DEC_BATCH = 128
DEC_SEQ = 1
PAST_LEN = 2048
PAGE_SIZE = 128

N_HEADS = 8
HEAD_DIM = D_MODEL // N_HEADS
BLOCK = 256
TOPK = 3
Q_CHUNK = 32
CONV_W = 31
D_FF = 2816
FFN_CONV_W = 3
N_ATTN = (DEPTH + 1) // 2
N_CONV = DEPTH // 2
ALPHA = (2.0 * DEPTH) ** 0.25
BETA = (8.0 * DEPTH) ** -0.25
LN_EPS = 1e-5
NEG = -1e30
SCALE = 1.0 / math.sqrt(HEAD_DIM)

kernel_name = 'moba_conformer_hybrid_step'


def _layer_norm(x, g, b):
    xf = x.astype(jnp.float32)
    mu = jnp.mean(xf, axis=-1, keepdims=True)
    var = jnp.mean(jnp.square(xf - mu), axis=-1, keepdims=True)
    return ((xf - mu) * lax.rsqrt(var + LN_EPS) * g + b).astype(x.dtype)


def _alibi_slopes():
    return jnp.exp2(-8.0 * (jnp.arange(N_HEADS, dtype=jnp.float32) + 1.0) / N_HEADS)


def _dwconv(x_ext, w, b):
    y = lax.conv_general_dilated(x_ext, w[:, None, :].astype(x_ext.dtype), window_strides=(1,), padding='VALID',
                                 dimension_numbers=('NWC', 'WIO', 'NWC'), feature_group_count=x_ext.shape[-1])
    return y + b


def _pad_blocks(t):
    L = t.shape[1]
    Lp = -(-L // BLOCK) * BLOCK
    return jnp.pad(t, ((0, 0), (0, Lp - L), (0, 0), (0, 0)))


def _qkv(x, w_qkv):
    N, S, _ = x.shape
    q, k, v = jnp.split(x @ w_qkv, 3, axis=-1)
    shp = (N, S, N_HEADS, HEAD_DIM)
    return q.reshape(shp), k.reshape(shp), v.reshape(shp)


def _moba(q, k_seq, v_seq, q_pos0, q_chunk):
    N, Q, H, Dh = q.shape
    nb = k_seq.shape[1] // BLOCK
    kb = k_seq.reshape(N, nb, BLOCK, H, Dh)
    vb = v_seq.reshape(N, nb, BLOCK, H, Dh)
    kmean = jnp.mean(kb.astype(jnp.float32), axis=2)
    if nb < TOPK:
        kmean = jnp.pad(kmean, ((0, 0), (0, TOPK - nb), (0, 0), (0, 0)))
    ng = kmean.shape[1]
    slopes = _alibi_slopes()
    n_chunks = Q // q_chunk
    qc_all = jnp.moveaxis(q.reshape(N, n_chunks, q_chunk, H, Dh), 1, 0)
    n_idx = jnp.arange(N)[:, None, None, None]
    h_idx = jnp.arange(H)[None, None, :, None]

    def chunk_fn(args):
        qch, c = args
        t = q_pos0 + c * q_chunk + jnp.arange(q_chunk, dtype=jnp.int32)
        own = t // BLOCK
        gate = jnp.einsum('nqhd,nbhd->nqhb', qch.astype(jnp.float32), kmean)
        past = jnp.arange(ng)[None, None, None, :] < own[None, :, None, None]
        gate = jnp.where(past, gate, NEG)
        _, top = lax.top_k(gate, TOPK)
        own_b = jnp.broadcast_to(own[None, :, None, None], (N, q_chunk, H, 1))
        top_valid = top < own_b
        blk = jnp.concatenate([jnp.where(top_valid, top, own_b), own_b], axis=-1)
        valid = jnp.concatenate([top_valid, jnp.ones_like(own_b, dtype=bool)], axis=-1)
        kg = kb[n_idx, blk, :, h_idx, :]
        vg = vb[n_idx, blk, :, h_idx, :]
        s_pos = blk[..., None] * BLOCK + jnp.arange(BLOCK, dtype=jnp.int32)
        dist = t[None, :, None, None, None] - s_pos
        mask = valid[..., None] & (dist >= 0)
        scores = (jnp.einsum('nqhd,nqhjsd->nqhjs', qch, kg, preferred_element_type=jnp.float32) * SCALE
                  - slopes[None, None, :, None, None] * dist.astype(jnp.float32))
        scores = jnp.where(mask, scores, NEG)
        p = jax.nn.softmax(scores.reshape(N, q_chunk, H, -1), axis=-1).reshape(scores.shape).astype(vg.dtype)
        return jnp.einsum('nqhjs,nqhjsd->nqhd', p, vg)

    out = lax.map(chunk_fn, (qc_all, jnp.arange(n_chunks, dtype=jnp.int32)))
    return jnp.moveaxis(out, 0, 1).reshape(N, Q, H, Dh)


def _conformer_conv(x, state, w_pw1, b_pw1, w_dw, b_dw, g, b, w_pw2, b_pw2):
    a, gt = jnp.split(x @ w_pw1 + b_pw1, 2, axis=-1)
    u = a * jax.nn.sigmoid(gt)
    u_ext = jnp.concatenate([state.astype(u.dtype), u], axis=1)
    y = _dwconv(u_ext, w_dw, b_dw)
    y = jax.nn.silu(_layer_norm(y, g, b))
    return y @ w_pw2 + b_pw2, u_ext[:, -(CONV_W - 1):]


def _conv_ffn(x, state, w_up, w_dw, b_dw, w_down):
    h = x @ w_up
    h_ext = jnp.concatenate([state.astype(h.dtype), h], axis=1)
    hc = _dwconv(h_ext, w_dw, b_dw)
    gt, u = jnp.split(hc, 2, axis=-1)
    return (jax.nn.silu(gt) * u) @ w_down, h_ext[:, -(FFN_CONV_W - 1):]


def setup_inputs(seed: int = 0) -> dict:
    key = jax.random.key(seed)
    ks = jax.random.split(key, 32)
    f32 = jnp.float32
    n_pages = PAST_LEN // PAGE_SIZE
    used = DEC_BATCH * n_pages
    n_pool = used + max(1, used // 4)
    D, F = D_MODEL, D_FF
    nrm = lambda k, shp, s: jax.random.normal(k, shp, f32) * s
    w_qkv = nrm(ks[0], (N_ATTN, D, 3 * D), D ** -0.5)
    w_qkv = w_qkv * jnp.concatenate([jnp.ones((2 * D,), f32), jnp.full((D,), BETA, f32)])
    return {
        'x_prompt': nrm(ks[1], (BATCH, SEQ, D), 1.0),
        'x_sample': nrm(ks[2], (DEC_BATCH, DEC_SEQ, D), 1.0),
        'cache_k': nrm(ks[3], (N_ATTN, n_pool, PAGE_SIZE, N_HEADS, HEAD_DIM), 1.0),
        'cache_v': nrm(ks[4], (N_ATTN, n_pool, PAGE_SIZE, N_HEADS, HEAD_DIM), 1.0),
        'state_conv': nrm(ks[5], (N_CONV, DEC_BATCH, CONV_W - 1, D), 0.5),
        'state_ffn': nrm(ks[6], (DEPTH, DEC_BATCH, FFN_CONV_W - 1, 2 * F), 1.0),
        'page_table': jax.random.permutation(ks[7], n_pool)[:used].reshape(DEC_BATCH, n_pages).astype(jnp.int32),
        'w_qkv': w_qkv,
        'w_o': nrm(ks[8], (N_ATTN, D, D), BETA * D ** -0.5),
        'w_pw1': nrm(ks[9], (N_CONV, D, 2 * D), D ** -0.5),
        'b_pw1': nrm(ks[10], (N_CONV, 2 * D), 0.02),
        'w_dw': nrm(ks[11], (N_CONV, CONV_W, D), CONV_W ** -0.5),
        'b_dw': nrm(ks[12], (N_CONV, D), 0.02),
        'ln_cv_g': 1.0 + nrm(ks[13], (N_CONV, D), 0.02),
        'ln_cv_b': nrm(ks[14], (N_CONV, D), 0.02),
        'w_pw2': nrm(ks[15], (N_CONV, D, D), BETA * D ** -0.5),
        'b_pw2': nrm(ks[16], (N_CONV, D), 0.02),
        'w_up': nrm(ks[17], (DEPTH, D, 2 * F), D ** -0.5),
        'w_fdw': nrm(ks[18], (DEPTH, FFN_CONV_W, 2 * F), FFN_CONV_W ** -0.5),
        'b_fdw': nrm(ks[19], (DEPTH, 2 * F), 0.02),
        'w_down': nrm(ks[20], (DEPTH, F, D), BETA * F ** -0.5),
        'ln1_g': 1.0 + nrm(ks[21], (DEPTH, D), 0.02),
        'ln1_b': nrm(ks[22], (DEPTH, D), 0.02),
        'ln2_g': 1.0 + nrm(ks[23], (DEPTH, D), 0.02),
        'ln2_b': nrm(ks[24], (DEPTH, D), 0.02),
    }


def reference(x_prompt, x_sample, cache_k, cache_v, state_conv, state_ffn, page_table, w_qkv, w_o, w_pw1, b_pw1,
              w_dw, b_dw, ln_cv_g, ln_cv_b, w_pw2, b_pw2, w_up, w_fdw, b_fdw, w_down, ln1_g, ln1_b, ln2_g, ln2_b):
    yp, ys = x_prompt, x_sample
    Bp, Sp, D = yp.shape
    Bs, Ss, _ = ys.shape
    past_len = page_table.shape[1] * cache_k.shape[2]
    k_pr, v_pr, k_sa, v_sa, c_pr, c_sa, f_pr, f_sa = [], [], [], [], [], [], [], []
    for i in range(DEPTH):
        j = i // 2
        if i % 2 == 0:
            qp, kp, vp = _qkv(yp, w_qkv[j])
            op = _moba(qp, _pad_blocks(kp), _pad_blocks(vp), 0, Q_CHUNK)
            qs, ks_, vs = _qkv(ys, w_qkv[j])
            k_past = cache_k[j, page_table].reshape(Bs, past_len, N_HEADS, HEAD_DIM)
            v_past = cache_v[j, page_table].reshape(Bs, past_len, N_HEADS, HEAD_DIM)
            k_seq = _pad_blocks(jnp.concatenate([k_past.astype(ks_.dtype), ks_], axis=1))
            v_seq = _pad_blocks(jnp.concatenate([v_past.astype(vs.dtype), vs], axis=1))
            os_ = _moba(qs, k_seq, v_seq, past_len, Ss)
            mp = op.reshape(Bp, Sp, D) @ w_o[j]
            ms = os_.reshape(Bs, Ss, D) @ w_o[j]
            k_pr.append(kp); v_pr.append(vp); k_sa.append(ks_); v_sa.append(vs)
        else:
            zp = jnp.zeros((Bp, CONV_W - 1, D), yp.dtype)
            mp, cp = _conformer_conv(yp, zp, w_pw1[j], b_pw1[j], w_dw[j], b_dw[j], ln_cv_g[j], ln_cv_b[j], w_pw2[j], b_pw2[j])
            ms, cs = _conformer_conv(ys, state_conv[j], w_pw1[j], b_pw1[j], w_dw[j], b_dw[j], ln_cv_g[j], ln_cv_b[j], w_pw2[j], b_pw2[j])
            c_pr.append(cp); c_sa.append(cs)
        yp = _layer_norm(ALPHA * yp + mp, ln1_g[i], ln1_b[i])
        ys = _layer_norm(ALPHA * ys + ms, ln1_g[i], ln1_b[i])
        zf = jnp.zeros((Bp, FFN_CONV_W - 1, w_up.shape[-1]), yp.dtype)
        fp, sp = _conv_ffn(yp, zf, w_up[i], w_fdw[i], b_fdw[i], w_down[i])
        fs, ss = _conv_ffn(ys, state_ffn[i], w_up[i], w_fdw[i], b_fdw[i], w_down[i])
        f_pr.append(sp); f_sa.append(ss)
        yp = _layer_norm(ALPHA * yp + fp, ln2_g[i], ln2_b[i])
        ys = _layer_norm(ALPHA * ys + fs, ln2_g[i], ln2_b[i])
    return (yp, ys, jnp.stack(k_pr), jnp.stack(v_pr), jnp.stack(k_sa), jnp.stack(v_sa),
            jnp.stack(c_pr), jnp.stack(c_sa), jnp.stack(f_pr), jnp.stack(f_sa))
```

```python
import functools
import math

import numpy as np
import jax
import jax.numpy as jnp
from jax import lax
from jax.experimental import pallas as pl
from jax.experimental.pallas import tpu as pltpu

N_HEADS = 8
HEAD_DIM = 128
D_MODEL = N_HEADS * HEAD_DIM
BLOCK = 256
TOPK = 3
CONV_W = 31
FFN_CONV_W = 3
DEPTH = 4
ALPHA = (2.0 * DEPTH) ** 0.25
LN_EPS = 1e-5
NEG = -1e30
SCALE = 1.0 / math.sqrt(HEAD_DIM)

LANES = 128
VMEM_LIMIT = 56 * 1024 * 1024

F32 = jnp.float32
BF16 = jnp.bfloat16

TM_ROWS = 512
TM_CONV = 256
CONV_HALO = 32
CONV_R = 32
CONV_C = 256
TM_FFN = 1024
FC_FFN = 256
TB_CONV_S = 32

_NT = (((1,), (1,)), ((), ()))


def _alibi_slopes():
    s = np.exp2(-8.0 * (np.arange(N_HEADS, dtype=np.float64) + 1.0) / N_HEADS).astype(np.float32)
    assert np.all(s.astype(jnp.bfloat16).astype(np.float32) == s)
    assert np.all(np.log2(s) == np.round(np.log2(s)))
    return s


def _params(*sem):
    return pltpu.CompilerParams(dimension_semantics=sem, vmem_limit_bytes=VMEM_LIMIT)


def _layer_norm(x, g, b):
    mu = jnp.mean(x, axis=-1, keepdims=True)
    xc = x - mu
    var = jnp.mean(xc * xc, axis=-1, keepdims=True)
    return xc * lax.rsqrt(var + LN_EPS) * g + b


def _silu(x):
    return x * jax.nn.sigmoid(x)


def _qkv_prompt_kernel(x_ref, w_ref, q_ref, qs_ref, k_ref, v_ref, kf_ref, vf_ref, km_ref):
    tm = x_ref.shape[0]
    xb = x_ref[...].astype(BF16)
    q = jnp.dot(xb, w_ref[:, 0:D_MODEL], preferred_element_type=F32)
    qs = q * SCALE
    for h in range(N_HEADS):
        sl = slice(h * HEAD_DIM, (h + 1) * HEAD_DIM)
        q_ref[h] = q[:, sl].astype(BF16)
        qs_ref[h] = qs[:, sl].astype(BF16)
    k = jnp.dot(xb, w_ref[:, D_MODEL:2 * D_MODEL], preferred_element_type=F32)
    kf_ref[...] = k
    for h in range(N_HEADS):
        k_ref[h] = k[:, h * HEAD_DIM:(h + 1) * HEAD_DIM].astype(BF16)
    for j in range(tm // BLOCK):
        km_ref[j] = jnp.mean(k[j * BLOCK:(j + 1) * BLOCK], axis=0, keepdims=True)
    v = jnp.dot(xb, w_ref[:, 2 * D_MODEL:3 * D_MODEL], preferred_element_type=F32)
    vf_ref[...] = v
    for h in range(N_HEADS):
        v_ref[h] = v[:, h * HEAD_DIM:(h + 1) * HEAD_DIM].astype(BF16)


def _qkv_prompt(x, w, nb_batch, seq):
    m, d = x.shape
    tm = TM_ROWS
    assert seq % tm == 0 and tm % BLOCK == 0 and d == D_MODEL
    nt = seq // tm
    hm = jax.ShapeDtypeStruct((nb_batch, N_HEADS, seq, HEAD_DIM), BF16)
    hm_spec = pl.BlockSpec((None, N_HEADS, tm, HEAD_DIM), lambda b, i: (b, 0, i, 0))
    row_spec = pl.BlockSpec((tm, d), lambda b, i: (b * nt + i, 0))
    return pl.pallas_call(
        _qkv_prompt_kernel,
        grid=(nb_batch, nt),
        in_specs=[row_spec, pl.BlockSpec((d, 3 * d), lambda b, i: (0, 0))],
        out_specs=[hm_spec, hm_spec, hm_spec, hm_spec, row_spec, row_spec,
                   pl.BlockSpec((tm // BLOCK, 1, d), lambda b, i: (b * nt + i, 0, 0))],
        out_shape=[hm, hm, hm, hm,
                   jax.ShapeDtypeStruct((m, d), F32), jax.ShapeDtypeStruct((m, d), F32),
                   jax.ShapeDtypeStruct((m // BLOCK, 1, d), F32)],
        compiler_params=_params("parallel", "parallel"),
        name="qkv_prompt",
    )(x, w)


def _mm_kernel(x_ref, w_ref, o_ref):
    o_ref[...] = jnp.dot(x_ref[...].astype(BF16), w_ref[...], preferred_element_type=F32)


def _mm(x, w, tn=1024):
    m, k = x.shape
    n = w.shape[1]
    assert n % tn == 0
    return pl.pallas_call(
        _mm_kernel,
        grid=(n // tn,),
        in_specs=[pl.BlockSpec((m, k), lambda j: (0, 0)), pl.BlockSpec((k, tn), lambda j: (0, j))],
        out_specs=pl.BlockSpec((m, tn), lambda j: (0, j)),
        out_shape=jax.ShapeDtypeStruct((m, n), F32),
        compiler_params=_params("parallel"),
        name="mm_rows",
    )(x, w)


def _proj_ln_kernel(has_bias, a_ref, w_ref, *refs):
    if has_bias:
        bias_ref, y_ref, g_ref, b_ref, o_ref = refs
    else:
        y_ref, g_ref, b_ref, o_ref = refs
    mp = jnp.dot(a_ref[...].astype(BF16), w_ref[...], preferred_element_type=F32)
    if has_bias:
        mp = mp + bias_ref[...]
    o_ref[...] = _layer_norm(ALPHA * y_ref[...] + mp, g_ref[...], b_ref[...])


def _proj_ln(a, w, bias, y, g, b):
    m, d = y.shape
    tm = min(TM_ROWS, m)
    assert m % tm == 0
    row = lambda i: (i, 0)
    fixed = lambda i: (0, 0)
    vec = pl.BlockSpec((1, d), fixed)
    in_specs = [pl.BlockSpec((tm, a.shape[1]), row), pl.BlockSpec(w.shape, fixed)]
    args = [a, w]
    if bias is not None:
        in_specs.append(vec)
        args.append(bias.reshape(1, d))
    in_specs += [pl.BlockSpec((tm, d), row), vec, vec]
    args += [y, g.reshape(1, d), b.reshape(1, d)]
    return pl.pallas_call(
        functools.partial(_proj_ln_kernel, bias is not None),
        grid=(m // tm,),
        in_specs=in_specs,
        out_specs=pl.BlockSpec((tm, d), row),
        out_shape=jax.ShapeDtypeStruct((m, d), F32),
        compiler_params=_params("parallel"),
        name="proj_ln",
    )(*args)


def _pw1_glu_kernel(x_ref, w_ref, b_ref, u_ref):
    d = u_ref.shape[1]
    xb = x_ref[...].astype(BF16)
    a = jnp.dot(xb, w_ref[:, 0:d], preferred_element_type=F32) + b_ref[:, 0:d]
    gt = jnp.dot(xb, w_ref[:, d:2 * d], preferred_element_type=F32) + b_ref[:, d:2 * d]
    u_ref[...] = a * jax.nn.sigmoid(gt)


def _pw1_glu(x, w, bias):
    m, d = x.shape
    tm = min(TM_ROWS, m)
    assert m % tm == 0
    return pl.pallas_call(
        _pw1_glu_kernel,
        grid=(m // tm,),
        in_specs=[pl.BlockSpec((tm, d), lambda i: (i, 0)), pl.BlockSpec((d, 2 * d), lambda i: (0, 0)),
                  pl.BlockSpec((1, 2 * d), lambda i: (0, 0))],
        out_specs=pl.BlockSpec((tm, d), lambda i: (i, 0)),
        out_shape=jax.ShapeDtypeStruct((m, d), F32),
        compiler_params=_params("parallel"),
        name="pw1_glu",
    )(x, w, bias.reshape(1, 2 * d))


def _dwconv_ln_prompt_kernel(tiles_per_seq, u_ref, halo_ref, w_ref, b_ref, g_ref, be_ref, o_ref, ext_ref, y_ref):
    i = pl.program_id(0)
    tm, d = u_ref.shape
    first = (i % tiles_per_seq) == 0
    ext_ref[0:CONV_HALO, :] = jnp.where(first, 0.0, halo_ref[...])
    ext_ref[CONV_HALO:CONV_HALO + tm, :] = u_ref[...]
    off = CONV_HALO - (CONV_W - 1)

    def row_body(r, carry):
        r0 = pl.multiple_of(r * CONV_R, CONV_R)
        for c in range(d // CONV_C):
            cs = slice(c * CONV_C, (c + 1) * CONV_C)
            win = ext_ref[pl.ds(r0, CONV_R + CONV_HALO), cs]
            acc = jnp.zeros((CONV_R, CONV_C), F32)
            for k in range(CONV_W):
                acc = acc + win[off + k:off + k + CONV_R, :] * w_ref[k:k + 1, cs]
            y_ref[pl.ds(r0, CONV_R), cs] = acc + b_ref[:, cs]
        return carry

    lax.fori_loop(0, tm // CONV_R, row_body, 0)
    z = _layer_norm(y_ref[...], g_ref[...], be_ref[...])
    o_ref[...] = _silu(z).astype(BF16)


def _dwconv_ln_prompt(u, w, bias, g, b, seq):
    m, d = u.shape
    tm = TM_CONV
    assert seq % tm == 0 and tm % CONV_R == 0 and tm % CONV_HALO == 0 and d % CONV_C == 0
    assert CONV_HALO >= CONV_W - 1
    hb = tm // CONV_HALO
    vec = pl.BlockSpec((1, d), lambda i: (0, 0))
    return pl.pallas_call(
        functools.partial(_dwconv_ln_prompt_kernel, seq // tm),
        grid=(m // tm,),
        in_specs=[pl.BlockSpec((tm, d), lambda i: (i, 0)),
                  pl.BlockSpec((CONV_HALO, d), lambda i: (jnp.maximum(i * hb - 1, 0), 0)),
                  pl.BlockSpec((CONV_W, d), lambda i: (0, 0)), vec, vec, vec],
        out_specs=pl.BlockSpec((tm, d), lambda i: (i, 0)),
        out_shape=jax.ShapeDtypeStruct((m, d), BF16),
        scratch_shapes=[pltpu.VMEM((tm + CONV_HALO, d), F32), pltpu.VMEM((tm, d), F32)],
        compiler_params=_params("parallel"),
        name="dwconv_ln_prompt",
    )(u, u, w, bias.reshape(1, d), g.reshape(1, d), b.reshape(1, d))


def _dwconv_ln_sample_kernel(st_ref, u_ref, w_ref, b_ref, g_ref, be_ref, o_ref):
    st = st_ref[...]
    w = w_ref[...]
    y = jnp.sum(st * w[None, 0:CONV_W - 1, :], axis=1)
    y = y + u_ref[...] * w[CONV_W - 1:CONV_W, :] + b_ref[...]
    z = _layer_norm(y, g_ref[...], be_ref[...])
    o_ref[...] = _silu(z).astype(BF16)


def _dwconv_ln_sample(state, u, w, bias, g, b):
    nb, hist, d = state.shape
    tb = min(TB_CONV_S, nb)
    assert nb % tb == 0 and hist == CONV_W - 1
    vec = pl.BlockSpec((1, d), lambda i: (0, 0))
    return pl.pallas_call(
        _dwconv_ln_sample_kernel,
        grid=(nb // tb,),
        in_specs=[pl.BlockSpec((tb, hist, d), lambda i: (i, 0, 0)), pl.BlockSpec((tb, d), lambda i: (i, 0)),
                  pl.BlockSpec((CONV_W, d), lambda i: (0, 0)), vec, vec, vec],
        out_specs=pl.BlockSpec((tb, d), lambda i: (i, 0)),
        out_shape=jax.ShapeDtypeStruct((nb, d), BF16),
        compiler_params=_params("parallel"),
        name="dwconv_ln_sample",
    )(state, u, w, bias.reshape(1, d), g.reshape(1, d), b.reshape(1, d))


def _aug_lanes(nb):
    return dict(tq=nb, r=nb + 1, kb=nb + 2, qb=nb + 3)


def _moba_prompt_kernel(slopes_ref, q_ref, qs_ref, k_ref, v_ref, km_ref, o_ref, tab_ref, acc_ref):
    h = pl.program_id(1)
    qi = pl.program_id(2)
    nb = k_ref.shape[0] // BLOCK
    ln = _aug_lanes(nb)
    slope = slopes_ref[h]
    lane = lax.broadcasted_iota(jnp.int32, (BLOCK, HEAD_DIM), 1)
    rowf = lax.broadcasted_iota(jnp.int32, (BLOCK, HEAD_DIM), 0).astype(F32)

    @pl.when(qi == 0)
    def _build_key_table():
        base = jnp.where((lane == ln["tq"]) | (lane == ln["qb"]), 1.0, 0.0)
        base = jnp.where(lane == ln["r"], slope * rowf, base)

        def tab_body(b, carry):
            t = jnp.where(lane == b, 1.0, base)
            t = jnp.where(lane == ln["kb"], slope * (BLOCK * b).astype(F32), t)
            tab_ref[pl.ds(pl.multiple_of(b * BLOCK, BLOCK), BLOCK), :] = t.astype(BF16)
            return carry

        lax.fori_loop(0, nb, tab_body, 0)

    km = jnp.concatenate([km_ref[...].astype(BF16), jnp.zeros((HEAD_DIM - nb, HEAD_DIM), BF16)], axis=0)
    gate = lax.dot_general(q_ref[...], km, _NT, preferred_element_type=F32)
    past = lane < qi
    gw = jnp.where(past, gate, -jnp.inf)
    valid_bias = jnp.where(past, 0.0, NEG)
    selbias = jnp.full((BLOCK, HEAD_DIM), NEG, F32)
    lanef = lane.astype(F32)
    for _ in range(TOPK):
        mx = jnp.max(gw, axis=1, keepdims=True)
        idx = jnp.min(jnp.where(gw == mx, lanef, float(HEAD_DIM)), axis=1, keepdims=True)
        pick = lanef == idx
        selbias = jnp.where(pick, valid_bias, selbias)
        gw = jnp.where(pick, -jnp.inf, gw)
    selbias = jnp.where(lane == qi, 0.0, selbias)

    aug = jnp.where(lane < nb, selbias, 0.0)
    aug = jnp.where(lane == ln["tq"], -slope * rowf, aug)
    aug = jnp.where((lane == ln["r"]) | (lane == ln["kb"]), 1.0, aug)
    aug = jnp.where(lane == ln["qb"], -slope * (BLOCK * qi).astype(F32), aug)
    qa = jnp.concatenate([qs_ref[...], aug.astype(BF16)], axis=1)

    def key_block(b):
        rows = pl.ds(pl.multiple_of(b * BLOCK, BLOCK), BLOCK)
        return jnp.concatenate([k_ref[rows, :], tab_ref[rows, :]], axis=1), v_ref[rows, :]

    ka, vb = key_block(qi)
    s = lax.dot_general(qa, ka, _NT, preferred_element_type=F32)
    col = lax.broadcasted_iota(jnp.int32, (BLOCK, BLOCK), 1)
    row = lax.broadcasted_iota(jnp.int32, (BLOCK, BLOCK), 0)
    s = jnp.where(col <= row, s, NEG)
    m0 = jnp.max(s, axis=1, keepdims=True)
    p = jnp.exp(s - m0)
    l0 = jnp.sum(p, axis=1, keepdims=True)
    acc_ref[...] = jnp.dot(p.astype(BF16), vb, preferred_element_type=F32)

    def blk_body(b, carry):
        m, l = carry
        ka, vb = key_block(b)
        s = lax.dot_general(qa, ka, _NT, preferred_element_type=F32)
        m_new = jnp.maximum(m, jnp.max(s, axis=1, keepdims=True))
        a = jnp.exp(m - m_new)
        p = jnp.exp(s - m_new)
        l = a * l + jnp.sum(p, axis=1, keepdims=True)
        acc_ref[...] = a * acc_ref[...] + jnp.dot(p.astype(BF16), vb, preferred_element_type=F32)
        return m_new, l

    m, l = lax.fori_loop(0, qi, blk_body, (m0, l0))
    o_ref[...] = (acc_ref[...] / l).astype(BF16)


def _moba_prompt(q, qs, k, v, km, slopes):
    nbt, nh, seq, hd = q.shape
    assert seq % BLOCK == 0 and hd == HEAD_DIM == LANES and nh == N_HEADS
    nq = seq // BLOCK
    assert nq + 4 <= HEAD_DIM
    qspec = pl.BlockSpec((None, None, BLOCK, hd), lambda b, h, i: (b, h, i, 0))
    kspec = pl.BlockSpec((None, None, seq, hd), lambda b, h, i: (b, h, 0, 0))
    return pl.pallas_call(
        _moba_prompt_kernel,
        grid=(nbt, nh, nq),
        in_specs=[pl.BlockSpec(memory_space=pltpu.SMEM), qspec, qspec, kspec, kspec,
                  pl.BlockSpec((None, nq, hd), lambda b, h, i: (b, 0, h))],
        out_specs=pl.BlockSpec((BLOCK, hd), lambda b, h, i: (b * nq + i, h)),
        out_shape=jax.ShapeDtypeStruct((nbt * seq, nh * hd), BF16),
        scratch_shapes=[pltpu.VMEM((seq, hd), BF16), pltpu.VMEM((BLOCK, hd), F32)],
        compiler_params=_params("parallel", "parallel", "arbitrary"),
        name="moba_prompt",
    )(slopes, q, qs, k, v, km)


def _moba_sample_kernel(n_pages, pt_ref, slopes_ref, q_ref, kn_ref, vn_ref, *refs):
    del pt_ref
    k_refs = refs[:n_pages]
    v_refs = refs[n_pages:2 * n_pages]
    o_ref = refs[2 * n_pages]
    page = k_refs[0].shape[0]
    d = q_ref.shape[1]
    past_len = n_pages * page
    nbp = past_len // BLOCK
    lane_d = lax.broadcasted_iota(jnp.int32, (N_HEADS, d), 1)
    row_d = lax.broadcasted_iota(jnp.int32, (N_HEADS, d), 0)
    head_mask = (lane_d // HEAD_DIM) == row_d
    qh = jnp.where(head_mask, q_ref[...], 0.0)
    row_p = lax.broadcasted_iota(jnp.int32, (N_HEADS, past_len), 0)

    raw = jnp.zeros((N_HEADS, past_len), F32)
    for h in range(N_HEADS):
        kh = jnp.concatenate([k_refs[p][:, h, :] for p in range(n_pages)], axis=0).astype(BF16)
        raw = raw + lax.dot_general(qh[:, h * HEAD_DIM:(h + 1) * HEAD_DIM].astype(BF16), kh, _NT,
                                    preferred_element_type=F32)

    lane = lax.broadcasted_iota(jnp.int32, (N_HEADS, LANES), 1)
    gw = jnp.full((N_HEADS, LANES), -jnp.inf, F32)
    for b in range(nbp):
        gb = jnp.sum(raw[:, b * BLOCK:(b + 1) * BLOCK], axis=1, keepdims=True) * (1.0 / BLOCK)
        gw = jnp.where(lane == b, gb, gw)
    sel = jnp.zeros((N_HEADS, LANES), F32)
    lanef = lane.astype(F32)
    for _ in range(min(TOPK, nbp)):
        mx = jnp.max(gw, axis=1, keepdims=True)
        idx = jnp.min(jnp.where(gw == mx, lanef, float(LANES)), axis=1, keepdims=True)
        pick = lanef == idx
        sel = jnp.where(pick, 1.0, sel)
        gw = jnp.where(pick, -jnp.inf, gw)

    slopes = slopes_ref[...]
    pos = lax.broadcasted_iota(jnp.int32, (N_HEADS, BLOCK), 1)
    parts = []
    for b in range(nbp):
        dist = (past_len - b * BLOCK - pos).astype(F32)
        sc = raw[:, b * BLOCK:(b + 1) * BLOCK] * SCALE - slopes * dist
        parts.append(jnp.where(sel[:, b:b + 1] > 0.5, sc, NEG))
    scores = jnp.concatenate(parts, axis=1)
    s_new = jnp.sum(qh * kn_ref[...], axis=1, keepdims=True) * SCALE

    m = jnp.maximum(jnp.max(scores, axis=1, keepdims=True), s_new)
    p = jnp.exp(scores - m)
    p_new = jnp.exp(s_new - m)
    l = jnp.sum(p, axis=1, keepdims=True) + p_new
    outs = []
    for h in range(N_HEADS):
        vh = jnp.concatenate([v_refs[pg][:, h, :] for pg in range(n_pages)], axis=0).astype(BF16)
        ph = jnp.where(row_p == h, p, 0.0).astype(BF16)
        outs.append(jnp.dot(ph, vh, preferred_element_type=F32))
    acc = jnp.concatenate(outs, axis=1)
    acc = (acc + jnp.where(head_mask, p_new * vn_ref[...], 0.0)) / l
    o_ref[...] = jnp.sum(acc, axis=0, keepdims=True)


def _moba_sample(q, k_new, v_new, cache_k, cache_v, page_table, layer, slopes):
    nb, d = q.shape
    n_pages = page_table.shape[1]
    page = cache_k.shape[2]
    assert (n_pages * page) % BLOCK == 0 and d == D_MODEL and cache_k.shape[3:] == (N_HEADS, HEAD_DIM)
    ck, cv = cache_k, cache_v
    row = pl.BlockSpec((None, 1, d), lambda b, pt: (b, 0, 0))

    def page_spec(p):
        return pl.BlockSpec((None, None, page, N_HEADS, HEAD_DIM), lambda b, pt: (layer, pt[b, p], 0, 0, 0))

    out = pl.pallas_call(
        functools.partial(_moba_sample_kernel, n_pages),
        grid_spec=pltpu.PrefetchScalarGridSpec(
            num_scalar_prefetch=1,
            grid=(nb,),
            in_specs=[pl.BlockSpec((N_HEADS, 1), lambda b, pt: (0, 0)), row, row, row]
                     + [page_spec(p) for p in range(n_pages)] * 2,
            out_specs=row,
        ),
        out_shape=jax.ShapeDtypeStruct((nb, 1, d), F32),
        compiler_params=_params("parallel"),
        name="moba_sample",
    )(page_table, slopes.reshape(N_HEADS, 1), q.reshape(nb, 1, d), k_new.reshape(nb, 1, d),
      v_new.reshape(nb, 1, d), *([ck] * n_pages), *([cv] * n_pages))
    return out.reshape(nb, d)


def _ffn_prompt_kernel(tiles_per_seq, x_ref, wg_ref, wu_ref, fwg_ref, fwu_ref, fbg_ref, fbu_ref, wd_ref,
                       g_ref, b_ref, y_ref, st_ref, xb_ref, acc_ref, hsg_ref, hsu_ref, cg_ref, cu_ref):
    i = pl.program_id(0)
    c = pl.program_id(1)
    tm = x_ref.shape[0]
    first = (i % tiles_per_seq) == 0

    @pl.when(c == 0)
    def _():
        xb_ref[...] = x_ref[...].astype(BF16)
        acc_ref[...] = jnp.zeros_like(acc_ref)

    xb = xb_ref[...]

    def half(w_ref, fw_ref, fb_ref, hs_ref, carry_ref):
        hcur = jnp.dot(xb, w_ref[...], preferred_element_type=F32)
        @pl.when(first)
        def _():
            hs_ref[0:8, :] = jnp.zeros((8, hs_ref.shape[1]), F32)

        @pl.when(jnp.logical_not(first))
        def _():
            hs_ref[0:8, :] = carry_ref[c]

        hs_ref[8:8 + tm, :] = hcur
        carry_ref[c] = hcur[tm - 8:tm, :]
        conv = (fw_ref[0:1, :] * hs_ref[6:6 + tm, :] + fw_ref[1:2, :] * hs_ref[7:7 + tm, :]
                + fw_ref[2:3, :] * hcur + fb_ref[...])
        return conv, hcur[tm - (FFN_CONV_W - 1):tm, :]

    gt, tail_g = half(wg_ref, fwg_ref, fbg_ref, hsg_ref, cg_ref)
    u, tail_u = half(wu_ref, fwu_ref, fbu_ref, hsu_ref, cu_ref)
    for r in range(FFN_CONV_W - 1):
        st_ref[r] = jnp.concatenate([tail_g[r:r + 1, :], tail_u[r:r + 1, :]], axis=0)
    act = (_silu(gt) * u).astype(BF16)
    acc_ref[...] += jnp.dot(act, wd_ref[...], preferred_element_type=F32)

    @pl.when(c == pl.num_programs(1) - 1)
    def _():
        y_ref[...] = _layer_norm(ALPHA * x_ref[...] + acc_ref[...], g_ref[...], b_ref[...])


def _ffn_prompt(x, w_up, w_fdw, b_fdw, w_down, g, b, nb_batch, seq):
    m, d = x.shape
    f = w_down.shape[0]
    tm, fc = TM_FFN, FC_FFN
    assert seq % tm == 0 and f % fc == 0 and FFN_CONV_W == 3 and tm >= 8
    nc = f // fc
    tps = seq // tm
    vec = pl.BlockSpec((1, d), lambda i, c: (0, 0))
    y, st = pl.pallas_call(
        functools.partial(_ffn_prompt_kernel, tps),
        grid=(m // tm, nc),
        in_specs=[pl.BlockSpec((tm, d), lambda i, c: (i, 0)),
                  pl.BlockSpec((d, fc), lambda i, c: (0, c)), pl.BlockSpec((d, fc), lambda i, c: (0, nc + c)),
                  pl.BlockSpec((FFN_CONV_W, fc), lambda i, c: (0, c)),
                  pl.BlockSpec((FFN_CONV_W, fc), lambda i, c: (0, nc + c)),
                  pl.BlockSpec((1, fc), lambda i, c: (0, c)), pl.BlockSpec((1, fc), lambda i, c: (0, nc + c)),
                  pl.BlockSpec((fc, d), lambda i, c: (c, 0)), vec, vec],
        out_specs=[pl.BlockSpec((tm, d), lambda i, c: (i, 0)),
                   pl.BlockSpec((None, FFN_CONV_W - 1, 2, fc), lambda i, c: (i, 0, 0, c))],
        out_shape=[jax.ShapeDtypeStruct((m, d), F32),
                   jax.ShapeDtypeStruct((m // tm, FFN_CONV_W - 1, 2, f), F32)],
        scratch_shapes=[pltpu.VMEM((tm, d), BF16), pltpu.VMEM((tm, d), F32),
                        pltpu.VMEM((tm + 8, fc), F32), pltpu.VMEM((tm + 8, fc), F32),
                        pltpu.VMEM((nc, 8, fc), F32), pltpu.VMEM((nc, 8, fc), F32)],
        compiler_params=_params("arbitrary", "arbitrary"),
        name="ffn_prompt",
    )(x, w_up, w_up, w_fdw, w_fdw, b_fdw.reshape(1, 2 * f), b_fdw.reshape(1, 2 * f), w_down,
      g.reshape(1, d), b.reshape(1, d))
    return y, st[tps - 1::tps].reshape(nb_batch, FFN_CONV_W - 1, 2 * f)


def _ffn_sample_kernel(x_ref, s0g_ref, s0u_ref, s1g_ref, s1u_ref, wg_ref, wu_ref, fwg_ref, fwu_ref,
                       fbg_ref, fbu_ref, wd_ref, g_ref, b_ref, y_ref, hg_ref, hu_ref, acc_ref):
    c = pl.program_id(0)

    @pl.when(c == 0)
    def _():
        acc_ref[...] = jnp.zeros_like(acc_ref)

    xb = x_ref[...].astype(BF16)
    hg = jnp.dot(xb, wg_ref[...], preferred_element_type=F32)
    hu = jnp.dot(xb, wu_ref[...], preferred_element_type=F32)
    hg_ref[...] = hg
    hu_ref[...] = hu
    gt = fwg_ref[0:1, :] * s0g_ref[...] + fwg_ref[1:2, :] * s1g_ref[...] + fwg_ref[2:3, :] * hg + fbg_ref[...]
    u = fwu_ref[0:1, :] * s0u_ref[...] + fwu_ref[1:2, :] * s1u_ref[...] + fwu_ref[2:3, :] * hu + fbu_ref[...]
    act = (_silu(gt) * u).astype(BF16)
    acc_ref[...] += jnp.dot(act, wd_ref[...], preferred_element_type=F32)

    @pl.when(c == pl.num_programs(0) - 1)
    def _():
        y_ref[...] = _layer_norm(ALPHA * x_ref[...] + acc_ref[...], g_ref[...], b_ref[...])


def _ffn_sample(x, state, w_up, w_fdw, b_fdw, w_down, g, b):
    nb, d = x.shape
    f = w_down.shape[0]
    fc = FC_FFN
    assert f % fc == 0 and FFN_CONV_W == 3 and state.shape == (nb, FFN_CONV_W - 1, 2 * f)
    nc = f // fc
    st = state.reshape(nb, 4 * f)
    vec = pl.BlockSpec((1, d), lambda c: (0, 0))
    col = lambda k: pl.BlockSpec((nb, fc), lambda c: (0, k * nc + c))
    y, hg, hu = pl.pallas_call(
        _ffn_sample_kernel,
        grid=(nc,),
        in_specs=[pl.BlockSpec((nb, d), lambda c: (0, 0)), col(0), col(1), col(2), col(3),
                  pl.BlockSpec((d, fc), lambda c: (0, c)), pl.BlockSpec((d, fc), lambda c: (0, nc + c)),
                  pl.BlockSpec((FFN_CONV_W, fc), lambda c: (0, c)),
                  pl.BlockSpec((FFN_CONV_W, fc), lambda c: (0, nc + c)),
                  pl.BlockSpec((1, fc), lambda c: (0, c)), pl.BlockSpec((1, fc), lambda c: (0, nc + c)),
                  pl.BlockSpec((fc, d), lambda c: (c, 0)), vec, vec],
        out_specs=[pl.BlockSpec((nb, d), lambda c: (0, 0)), pl.BlockSpec((nb, fc), lambda c: (0, c)),
                   pl.BlockSpec((nb, fc), lambda c: (0, c))],
        out_shape=[jax.ShapeDtypeStruct((nb, d), F32), jax.ShapeDtypeStruct((nb, f), F32),
                   jax.ShapeDtypeStruct((nb, f), F32)],
        scratch_shapes=[pltpu.VMEM((nb, d), F32)],
        compiler_params=_params("arbitrary"),
        name="ffn_sample",
    )(x, st, st, st, st, w_up, w_up, w_fdw, w_fdw, b_fdw.reshape(1, 2 * f), b_fdw.reshape(1, 2 * f), w_down,
      g.reshape(1, d), b.reshape(1, d))
    new_state = jnp.stack([state[:, 1, :], jnp.concatenate([hg, hu], axis=1)], axis=1)
    return y, new_state


def kernel(x_prompt, x_sample, cache_k, cache_v, state_conv, state_ffn, page_table, w_qkv, w_o, w_pw1, b_pw1,
           w_dw, b_dw, ln_cv_g, ln_cv_b, w_pw2, b_pw2, w_up, w_fdw, b_fdw, w_down, ln1_g, ln1_b, ln2_g, ln2_b):
    bp, sp, d = x_prompt.shape
    bs, ss, _ = x_sample.shape
    assert d == D_MODEL and ss == 1
    slopes = jnp.asarray(_alibi_slopes())
    yp = x_prompt.reshape(bp * sp, d)
    ys = x_sample.reshape(bs, d)
    k_pr, v_pr, k_sa, v_sa, c_pr, c_sa, f_pr, f_sa = [], [], [], [], [], [], [], []
    for i in range(DEPTH):
        j = i // 2
        if i % 2 == 0:
            wq = w_qkv[j].astype(BF16)
            wo = w_o[j].astype(BF16)
            q, qs, k, v, kf, vf, km = _qkv_prompt(yp, wq, bp, sp)
            op = _moba_prompt(q, qs, k, v, km.reshape(bp, sp // BLOCK, d), slopes)
            qkv_s = _mm(ys, wq)
            q_s, k_s, v_s = qkv_s[:, 0:d], qkv_s[:, d:2 * d], qkv_s[:, 2 * d:3 * d]
            os_ = _moba_sample(q_s, k_s, v_s, cache_k, cache_v, page_table, j, slopes)
            yp = _proj_ln(op, wo, None, yp, ln1_g[i], ln1_b[i])
            ys = _proj_ln(os_, wo, None, ys, ln1_g[i], ln1_b[i])
            k_pr.append(kf.reshape(bp, sp, N_HEADS, HEAD_DIM))
            v_pr.append(vf.reshape(bp, sp, N_HEADS, HEAD_DIM))
            k_sa.append(k_s.reshape(bs, ss, N_HEADS, HEAD_DIM))
            v_sa.append(v_s.reshape(bs, ss, N_HEADS, HEAD_DIM))
        else:
            w1 = w_pw1[j].astype(BF16)
            w2 = w_pw2[j].astype(BF16)
            up = _pw1_glu(yp, w1, b_pw1[j])
            zp = _dwconv_ln_prompt(up, w_dw[j], b_dw[j], ln_cv_g[j], ln_cv_b[j], sp)
            us = _pw1_glu(ys, w1, b_pw1[j])
            zs = _dwconv_ln_sample(state_conv[j], us, w_dw[j], b_dw[j], ln_cv_g[j], ln_cv_b[j])
            yp = _proj_ln(zp, w2, b_pw2[j], yp, ln1_g[i], ln1_b[i])
            ys = _proj_ln(zs, w2, b_pw2[j], ys, ln1_g[i], ln1_b[i])
            c_pr.append(up.reshape(bp, sp, d)[:, sp - (CONV_W - 1):, :])
            c_sa.append(jnp.concatenate([state_conv[j][:, 1:, :], us[:, None, :]], axis=1))
        wu = w_up[i].astype(BF16)
        wd = w_down[i].astype(BF16)
        yp, fp = _ffn_prompt(yp, wu, w_fdw[i], b_fdw[i], wd, ln2_g[i], ln2_b[i], bp, sp)
        ys, fs = _ffn_sample(ys, state_ffn[i], wu, w_fdw[i], b_fdw[i], wd, ln2_g[i], ln2_b[i])
        f_pr.append(fp)
        f_sa.append(fs)
    return (yp.reshape(bp, sp, d), ys.reshape(bs, ss, d), jnp.stack(k_pr), jnp.stack(v_pr), jnp.stack(k_sa),
            jnp.stack(v_sa), jnp.stack(c_pr), jnp.stack(c_sa), jnp.stack(f_pr), jnp.stack(f_sa))
```

```python
import functools
import math

import numpy as np
import jax
import jax.numpy as jnp
from jax import lax
from jax.experimental import pallas as pl
from jax.experimental.pallas import tpu as pltpu

N_HEADS = 8
HEAD_DIM = 128
D_MODEL = N_HEADS * HEAD_DIM
BLOCK = 256
TOPK = 3
CONV_W = 31
FFN_CONV_W = 3
DEPTH = 4
ALPHA = (2.0 * DEPTH) ** 0.25
LN_EPS = 1e-5
NEG = -1e30
SCALE = 1.0 / math.sqrt(HEAD_DIM)

LANES = 128
SUBLANES = 8
VMEM_LIMIT = 56 * 1024 * 1024

F32 = jnp.float32
BF16 = jnp.bfloat16

TM_ROWS = 512
TM_CONV = 256
CONV_HALO = 32
CONV_R = 64
CONV_C = 256
MOBA_QT = 512
MOBA_KT = 1024
TM_FFN = 512
FC_FFN = 1408
FFN_SUB = 2
FFN_R = 32
FFN_C = 128
FC_FFN_S = 256
TB_CONV_S = 32

_NT = (((1,), (1,)), ((), ()))


def _alibi_slopes():
    s = np.exp2(-8.0 * (np.arange(N_HEADS, dtype=np.float64) + 1.0) / N_HEADS).astype(np.float32)
    assert np.all(s.astype(jnp.bfloat16).astype(np.float32) == s)
    assert np.all(np.log2(s) == np.round(np.log2(s)))
    return s


def _params(*sem):
    return pltpu.CompilerParams(dimension_semantics=sem, vmem_limit_bytes=VMEM_LIMIT)


def _layer_norm(x, g, b):
    mu = jnp.mean(x, axis=-1, keepdims=True)
    xc = x - mu
    var = jnp.mean(xc * xc, axis=-1, keepdims=True)
    return xc * lax.rsqrt(var + LN_EPS) * g + b


def _silu(x):
    return x * jax.nn.sigmoid(x)


def _qkv_prompt_kernel(x_ref, w_ref, q_ref, qs_ref, k_ref, v_ref, kf_ref, vf_ref, km_ref):
    tm = x_ref.shape[0]
    xb = x_ref[...].astype(BF16)
    q = jnp.dot(xb, w_ref[:, 0:D_MODEL], preferred_element_type=F32)
    qs = q * SCALE
    for h in range(N_HEADS):
        sl = slice(h * HEAD_DIM, (h + 1) * HEAD_DIM)
        q_ref[h] = q[:, sl].astype(BF16)
        qs_ref[h] = qs[:, sl].astype(BF16)
    k = jnp.dot(xb, w_ref[:, D_MODEL:2 * D_MODEL], preferred_element_type=F32)
    kf_ref[...] = k
    for h in range(N_HEADS):
        k_ref[h] = k[:, h * HEAD_DIM:(h + 1) * HEAD_DIM].astype(BF16)
    for j in range(tm // BLOCK):
        km_ref[j] = jnp.mean(k[j * BLOCK:(j + 1) * BLOCK], axis=0, keepdims=True)
    v = jnp.dot(xb, w_ref[:, 2 * D_MODEL:3 * D_MODEL], preferred_element_type=F32)
    vf_ref[...] = v
    for h in range(N_HEADS):
        v_ref[h] = v[:, h * HEAD_DIM:(h + 1) * HEAD_DIM].astype(BF16)


def _qkv_prompt(x, w, nb_batch, seq):
    m, d = x.shape
    tm = TM_ROWS
    assert seq % tm == 0 and tm % BLOCK == 0 and d == D_MODEL
    nt = seq // tm
    hm = jax.ShapeDtypeStruct((nb_batch, N_HEADS, seq, HEAD_DIM), BF16)
    hm_spec = pl.BlockSpec((None, N_HEADS, tm, HEAD_DIM), lambda b, i: (b, 0, i, 0))
    row_spec = pl.BlockSpec((tm, d), lambda b, i: (b * nt + i, 0))
    return pl.pallas_call(
        _qkv_prompt_kernel,
        grid=(nb_batch, nt),
        in_specs=[row_spec, pl.BlockSpec((d, 3 * d), lambda b, i: (0, 0))],
        out_specs=[hm_spec, hm_spec, hm_spec, hm_spec, row_spec, row_spec,
                   pl.BlockSpec((tm // BLOCK, 1, d), lambda b, i: (b * nt + i, 0, 0))],
        out_shape=[hm, hm, hm, hm,
                   jax.ShapeDtypeStruct((m, d), F32), jax.ShapeDtypeStruct((m, d), F32),
                   jax.ShapeDtypeStruct((m // BLOCK, 1, d), F32)],
        compiler_params=_params("parallel", "parallel"),
        name="qkv_prompt",
    )(x, w)


def _mm_kernel(x_ref, w_ref, o_ref):
    o_ref[...] = jnp.dot(x_ref[...].astype(BF16), w_ref[...], preferred_element_type=F32)


def _mm(x, w, tn=1024):
    m, k = x.shape
    n = w.shape[1]
    assert n % tn == 0
    return pl.pallas_call(
        _mm_kernel,
        grid=(n // tn,),
        in_specs=[pl.BlockSpec((m, k), lambda j: (0, 0)), pl.BlockSpec((k, tn), lambda j: (0, j))],
        out_specs=pl.BlockSpec((m, tn), lambda j: (0, j)),
        out_shape=jax.ShapeDtypeStruct((m, n), F32),
        compiler_params=_params("parallel"),
        name="mm_rows",
    )(x, w)


def _proj_ln_kernel(has_bias, a_ref, w_ref, *refs):
    if has_bias:
        bias_ref, y_ref, g_ref, b_ref, o_ref = refs
    else:
        y_ref, g_ref, b_ref, o_ref = refs
    mp = jnp.dot(a_ref[...].astype(BF16), w_ref[...], preferred_element_type=F32)
    if has_bias:
        mp = mp + bias_ref[...]
    o_ref[...] = _layer_norm(ALPHA * y_ref[...] + mp, g_ref[...], b_ref[...])


def _proj_ln(a, w, bias, y, g, b):
    m, d = y.shape
    tm = min(TM_ROWS, m)
    assert m % tm == 0
    row = lambda i: (i, 0)
    fixed = lambda i: (0, 0)
    vec = pl.BlockSpec((1, d), fixed)
    in_specs = [pl.BlockSpec((tm, a.shape[1]), row), pl.BlockSpec(w.shape, fixed)]
    args = [a, w]
    if bias is not None:
        in_specs.append(vec)
        args.append(bias.reshape(1, d))
    in_specs += [pl.BlockSpec((tm, d), row), vec, vec]
    args += [y, g.reshape(1, d), b.reshape(1, d)]
    return pl.pallas_call(
        functools.partial(_proj_ln_kernel, bias is not None),
        grid=(m // tm,),
        in_specs=in_specs,
        out_specs=pl.BlockSpec((tm, d), row),
        out_shape=jax.ShapeDtypeStruct((m, d), F32),
        compiler_params=_params("parallel"),
        name="proj_ln",
    )(*args)


def _pw1_glu_kernel(x_ref, w_ref, b_ref, u_ref):
    d = u_ref.shape[1]
    xb = x_ref[...].astype(BF16)
    a = jnp.dot(xb, w_ref[:, 0:d], preferred_element_type=F32) + b_ref[:, 0:d]
    gt = jnp.dot(xb, w_ref[:, d:2 * d], preferred_element_type=F32) + b_ref[:, d:2 * d]
    u_ref[...] = a * jax.nn.sigmoid(gt)


def _pw1_glu(x, w, bias):
    m, d = x.shape
    tm = min(TM_ROWS, m)
    assert m % tm == 0
    return pl.pallas_call(
        _pw1_glu_kernel,
        grid=(m // tm,),
        in_specs=[pl.BlockSpec((tm, d), lambda i: (i, 0)), pl.BlockSpec((d, 2 * d), lambda i: (0, 0)),
                  pl.BlockSpec((1, 2 * d), lambda i: (0, 0))],
        out_specs=pl.BlockSpec((tm, d), lambda i: (i, 0)),
        out_shape=jax.ShapeDtypeStruct((m, d), F32),
        compiler_params=_params("parallel"),
        name="pw1_glu",
    )(x, w, bias.reshape(1, 2 * d))


def _dwconv_ln_prompt_kernel(tiles_per_seq, u_ref, halo_ref, w_ref, b_ref, g_ref, be_ref, o_ref, ext_ref, y_ref):
    i = pl.program_id(0)
    tm, d = u_ref.shape
    first = (i % tiles_per_seq) == 0
    ext_ref[0:CONV_HALO, :] = jnp.where(first, 0.0, halo_ref[...])
    ext_ref[CONV_HALO:CONV_HALO + tm, :] = u_ref[...]
    off = CONV_HALO - (CONV_W - 1)

    def row_body(r, carry):
        r0 = pl.multiple_of(r * CONV_R, CONV_R)
        for c in range(d // CONV_C):
            cs = slice(c * CONV_C, (c + 1) * CONV_C)
            win = ext_ref[pl.ds(r0, CONV_R + CONV_HALO), cs]
            acc = jnp.zeros((CONV_R, CONV_C), F32)
            for ph in range(SUBLANES):
                taps = [k for k in range(CONV_W) if (off + k) % SUBLANES == ph]
                if not taps:
                    continue
                wp = win if ph == 0 else pltpu.roll(win, CONV_R + CONV_HALO - ph, 0)
                for k in taps:
                    a = off + k - ph
                    acc = acc + wp[a:a + CONV_R, :] * w_ref[k:k + 1, cs]
            y_ref[pl.ds(r0, CONV_R), cs] = acc + b_ref[:, cs]
        return carry

    lax.fori_loop(0, tm // CONV_R, row_body, 0)
    z = _layer_norm(y_ref[...], g_ref[...], be_ref[...])
    o_ref[...] = _silu(z).astype(BF16)


def _dwconv_ln_prompt(u, w, bias, g, b, seq):
    m, d = u.shape
    tm = TM_CONV
    assert seq % tm == 0 and tm % CONV_R == 0 and tm % CONV_HALO == 0 and d % CONV_C == 0
    assert CONV_HALO >= CONV_W - 1 and CONV_HALO % SUBLANES == 0
    hb = tm // CONV_HALO
    vec = pl.BlockSpec((1, d), lambda i: (0, 0))
    return pl.pallas_call(
        functools.partial(_dwconv_ln_prompt_kernel, seq // tm),
        grid=(m // tm,),
        in_specs=[pl.BlockSpec((tm, d), lambda i: (i, 0)),
                  pl.BlockSpec((CONV_HALO, d), lambda i: (jnp.maximum(i * hb - 1, 0), 0)),
                  pl.BlockSpec((CONV_W, d), lambda i: (0, 0)), vec, vec, vec],
        out_specs=pl.BlockSpec((tm, d), lambda i: (i, 0)),
        out_shape=jax.ShapeDtypeStruct((m, d), BF16),
        scratch_shapes=[pltpu.VMEM((tm + CONV_HALO, d), F32), pltpu.VMEM((tm, d), F32)],
        compiler_params=_params("parallel"),
        name="dwconv_ln_prompt",
    )(u, u, w, bias.reshape(1, d), g.reshape(1, d), b.reshape(1, d))


def _dwconv_ln_sample_kernel(st_ref, u_ref, w_ref, b_ref, g_ref, be_ref, o_ref):
    st = st_ref[...]
    w = w_ref[...]
    y = jnp.sum(st * w[None, 0:CONV_W - 1, :], axis=1)
    y = y + u_ref[...] * w[CONV_W - 1:CONV_W, :] + b_ref[...]
    z = _layer_norm(y, g_ref[...], be_ref[...])
    o_ref[...] = _silu(z).astype(BF16)


def _dwconv_ln_sample(state, u, w, bias, g, b):
    nb, hist, d = state.shape
    tb = min(TB_CONV_S, nb)
    assert nb % tb == 0 and hist == CONV_W - 1
    vec = pl.BlockSpec((1, d), lambda i: (0, 0))
    return pl.pallas_call(
        _dwconv_ln_sample_kernel,
        grid=(nb // tb,),
        in_specs=[pl.BlockSpec((tb, hist, d), lambda i: (i, 0, 0)), pl.BlockSpec((tb, d), lambda i: (i, 0)),
                  pl.BlockSpec((CONV_W, d), lambda i: (0, 0)), vec, vec, vec],
        out_specs=pl.BlockSpec((tb, d), lambda i: (i, 0)),
        out_shape=jax.ShapeDtypeStruct((nb, d), BF16),
        compiler_params=_params("parallel"),
        name="dwconv_ln_sample",
    )(state, u, w, bias.reshape(1, d), g.reshape(1, d), b.reshape(1, d))


def _aug_lanes(nb):
    return dict(tq=nb, r=nb + 1, kb=nb + 2, qb=nb + 3)


def _moba_prompt_kernel(slopes_ref, q_ref, qs_ref, k_ref, v_ref, km_ref, o_ref, tab_ref, acc_ref):
    h = pl.program_id(1)
    i = pl.program_id(2)
    qt = q_ref.shape[0]
    nsub = qt // BLOCK
    nb = k_ref.shape[0] // BLOCK
    ln = _aug_lanes(nb)
    slope = slopes_ref[h]
    lane = lax.broadcasted_iota(jnp.int32, (qt, HEAD_DIM), 1)
    rowi = lax.broadcasted_iota(jnp.int32, (qt, HEAD_DIM), 0)

    @pl.when(i == 0)
    def _build_key_table():
        lane_b = lax.broadcasted_iota(jnp.int32, (BLOCK, HEAD_DIM), 1)
        rowf_b = lax.broadcasted_iota(jnp.int32, (BLOCK, HEAD_DIM), 0).astype(F32)
        base = jnp.where((lane_b == ln["tq"]) | (lane_b == ln["qb"]), 1.0, 0.0)
        base = jnp.where(lane_b == ln["r"], slope * rowf_b, base)

        def tab_body(b, carry):
            t = jnp.where(lane_b == b, 1.0, base)
            t = jnp.where(lane_b == ln["kb"], slope * (BLOCK * b).astype(F32), t)
            tab_ref[pl.ds(pl.multiple_of(b * BLOCK, BLOCK), BLOCK), :] = t.astype(BF16)
            return carry

        lax.fori_loop(0, nb, tab_body, 0)

    own = nsub * i + rowi // BLOCK
    km = jnp.concatenate([km_ref[...].astype(BF16), jnp.zeros((HEAD_DIM - nb, HEAD_DIM), BF16)], axis=0)
    gate = lax.dot_general(q_ref[...], km, _NT, preferred_element_type=F32)
    past = lane < own
    gw = jnp.where(past, gate, -jnp.inf)
    valid_bias = jnp.where(past, 0.0, NEG)
    selbias = jnp.full((qt, HEAD_DIM), NEG, F32)
    lanef = lane.astype(F32)
    for _ in range(TOPK):
        mx = jnp.max(gw, axis=1, keepdims=True)
        idx = jnp.min(jnp.where(gw == mx, lanef, float(HEAD_DIM)), axis=1, keepdims=True)
        pick = lanef == idx
        selbias = jnp.where(pick, valid_bias, selbias)
        gw = jnp.where(pick, -jnp.inf, gw)
    selbias = jnp.where(lane == own, 0.0, selbias)

    aug = jnp.where(lane < nb, selbias, 0.0)
    aug = jnp.where(lane == ln["tq"], -slope * (rowi % BLOCK).astype(F32), aug)
    aug = jnp.where((lane == ln["r"]) | (lane == ln["kb"]), 1.0, aug)
    aug = jnp.where(lane == ln["qb"], -slope * (BLOCK * own).astype(F32), aug)
    qa = jnp.concatenate([qs_ref[...], aug.astype(BF16)], axis=1)
    qas = [qa[a * BLOCK:(a + 1) * BLOCK, :] for a in range(nsub)]

    def key_tile(j):
        rows = pl.ds(pl.multiple_of(j * MOBA_KT, MOBA_KT), MOBA_KT)
        return jnp.concatenate([k_ref[rows, :], tab_ref[rows, :]], axis=1), v_ref[rows, :]

    jt = (nsub * i * BLOCK) // MOBA_KT
    ka, vb = key_tile(jt)
    diff = (lax.broadcasted_iota(jnp.int32, (BLOCK, MOBA_KT), 1)
            - lax.broadcasted_iota(jnp.int32, (BLOCK, MOBA_KT), 0))
    carry = []
    for a in range(nsub):
        s = lax.dot_general(qas[a], ka, _NT, preferred_element_type=F32)
        s = jnp.where(diff <= (nsub * i + a) * BLOCK - jt * MOBA_KT, s, NEG)
        m0 = jnp.max(s, axis=1, keepdims=True)
        p = jnp.exp(s - m0)
        carry.append((m0, jnp.sum(p, axis=1, keepdims=True)))
        acc_ref[a] = jnp.dot(p.astype(BF16), vb, preferred_element_type=F32)

    def tile_body(j, carry):
        ka, vb = key_tile(j)
        out = []
        for a in range(nsub):
            m, l = carry[a]
            s = lax.dot_general(qas[a], ka, _NT, preferred_element_type=F32)
            m_new = jnp.maximum(m, jnp.max(s, axis=1, keepdims=True))
            alpha = jnp.exp(m - m_new)
            p = jnp.exp(s - m_new)
            out.append((m_new, alpha * l + jnp.sum(p, axis=1, keepdims=True)))
            acc_ref[a] = alpha * acc_ref[a] + jnp.dot(p.astype(BF16), vb, preferred_element_type=F32)
        return tuple(out)

    carry = lax.fori_loop(0, jt, tile_body, tuple(carry))
    for a in range(nsub):
        o_ref[a * BLOCK:(a + 1) * BLOCK, :] = (acc_ref[a] / carry[a][1]).astype(BF16)


def _moba_prompt(q, qs, k, v, km, slopes):
    nbt, nh, seq, hd = q.shape
    qt = MOBA_QT
    assert seq % MOBA_KT == 0 and MOBA_KT % qt == 0 and qt % BLOCK == 0
    assert hd == HEAD_DIM == LANES and nh == N_HEADS
    nq = seq // qt
    nblk = seq // BLOCK
    assert nblk + 4 <= HEAD_DIM
    qspec = pl.BlockSpec((None, None, qt, hd), lambda b, h, i: (b, h, i, 0))
    kspec = pl.BlockSpec((None, None, seq, hd), lambda b, h, i: (b, h, 0, 0))
    return pl.pallas_call(
        _moba_prompt_kernel,
        grid=(nbt, nh, nq),
        in_specs=[pl.BlockSpec(memory_space=pltpu.SMEM), qspec, qspec, kspec, kspec,
                  pl.BlockSpec((None, nblk, hd), lambda b, h, i: (b, 0, h))],
        out_specs=pl.BlockSpec((qt, hd), lambda b, h, i: (b * nq + i, h)),
        out_shape=jax.ShapeDtypeStruct((nbt * seq, nh * hd), BF16),
        scratch_shapes=[pltpu.VMEM((seq, hd), BF16), pltpu.VMEM((qt // BLOCK, BLOCK, hd), F32)],
        compiler_params=_params("parallel", "parallel", "arbitrary"),
        name="moba_prompt",
    )(slopes, q, qs, k, v, km)


def _moba_sample_kernel(n_pages, pt_ref, slopes_ref, q_ref, kn_ref, vn_ref, *refs):
    del pt_ref
    k_refs = refs[:n_pages]
    v_refs = refs[n_pages:2 * n_pages]
    o_ref = refs[2 * n_pages]
    base_ref = refs[2 * n_pages + 1]
    page = k_refs[0].shape[0]
    past_len = n_pages * page
    nbp = past_len // BLOCK
    ppb = BLOCK // page
    lpp = page * N_HEADS
    lpb = BLOCK * N_HEADS
    lane_b = lax.broadcasted_iota(jnp.int32, (N_HEADS, lpb), 1)
    row_b = lax.broadcasted_iota(jnp.int32, (N_HEADS, lpb), 0)
    own_head = (lane_b % N_HEADS) == row_b

    @pl.when(pl.program_id(0) == 0)
    def _build_bias():
        slopes = slopes_ref[...]
        for b in range(nbp):
            dist = (past_len - b * BLOCK - lane_b // N_HEADS).astype(F32)
            base_ref[:, b * lpb:(b + 1) * lpb] = jnp.where(own_head, -slopes * dist, NEG)

    q8 = q_ref[...]
    qb = q8.astype(BF16)

    raws = []
    for b in range(nbp):
        parts = []
        for pg in range(b * ppb, (b + 1) * ppb):
            kf = k_refs[pg][...].reshape(lpp, HEAD_DIM).astype(BF16)
            parts.append(lax.dot_general(qb, kf, _NT, preferred_element_type=F32))
        raws.append(jnp.concatenate(parts, axis=1))

    lane = lax.broadcasted_iota(jnp.int32, (N_HEADS, LANES), 1)
    gw = jnp.full((N_HEADS, LANES), -jnp.inf, F32)
    for b in range(nbp):
        gb = jnp.sum(jnp.where(own_head, raws[b], 0.0), axis=1, keepdims=True) * (1.0 / BLOCK)
        gw = jnp.where(lane == b, gb, gw)
    sel = jnp.zeros((N_HEADS, LANES), F32)
    lanef = lane.astype(F32)
    for _ in range(min(TOPK, nbp)):
        mx = jnp.max(gw, axis=1, keepdims=True)
        idx = jnp.min(jnp.where(gw == mx, lanef, float(LANES)), axis=1, keepdims=True)
        pick = lanef == idx
        sel = jnp.where(pick, 1.0, sel)
        gw = jnp.where(pick, -jnp.inf, gw)

    scores = [jnp.where(sel[:, b:b + 1] > 0.5, raws[b] * SCALE + base_ref[:, b * lpb:(b + 1) * lpb], NEG)
              for b in range(nbp)]
    s_new = jnp.sum(q8 * kn_ref[...], axis=1, keepdims=True) * SCALE

    m = s_new
    for b in range(nbp):
        m = jnp.maximum(m, jnp.max(scores[b], axis=1, keepdims=True))
    p_new = jnp.exp(s_new - m)
    l = p_new
    acc = p_new * vn_ref[...]
    for b in range(nbp):
        p = jnp.exp(scores[b] - m)
        l = l + jnp.sum(p, axis=1, keepdims=True)
        pb = p.astype(BF16)
        for t in range(ppb):
            vf = v_refs[b * ppb + t][...].reshape(lpp, HEAD_DIM).astype(BF16)
            acc = acc + jnp.dot(pb[:, t * lpp:(t + 1) * lpp], vf, preferred_element_type=F32)
    o_ref[...] = acc / l


def _moba_sample(q, k_new, v_new, cache_k, cache_v, page_table, layer, slopes):
    nb, d = q.shape
    n_pages = page_table.shape[1]
    page = cache_k.shape[2]
    assert BLOCK % page == 0 and (n_pages * page) % BLOCK == 0 and d == D_MODEL
    assert cache_k.shape[3:] == (N_HEADS, HEAD_DIM) and N_HEADS == SUBLANES
    row = pl.BlockSpec((None, N_HEADS, HEAD_DIM), lambda b, pt: (b, 0, 0))

    def page_spec(p):
        return pl.BlockSpec((None, None, page, N_HEADS, HEAD_DIM), lambda b, pt: (layer, pt[b, p], 0, 0, 0))

    heads = lambda t: t.reshape(nb, N_HEADS, HEAD_DIM)
    out = pl.pallas_call(
        functools.partial(_moba_sample_kernel, n_pages),
        grid_spec=pltpu.PrefetchScalarGridSpec(
            num_scalar_prefetch=1,
            grid=(nb,),
            in_specs=[pl.BlockSpec((N_HEADS, 1), lambda b, pt: (0, 0)), row, row, row]
                     + [page_spec(p) for p in range(n_pages)] * 2,
            out_specs=row,
            scratch_shapes=[pltpu.VMEM((N_HEADS, n_pages * page * N_HEADS), F32)],
        ),
        out_shape=jax.ShapeDtypeStruct((nb, N_HEADS, HEAD_DIM), F32),
        compiler_params=_params("arbitrary"),
        name="moba_sample",
    )(page_table, slopes.reshape(N_HEADS, 1), heads(q), heads(k_new), heads(v_new),
      *([cache_k] * n_pages), *([cache_v] * n_pages))
    return out.reshape(nb, d)


def _ffn_prompt_kernel(tiles_per_seq, x_ref, wg_ref, wu_ref, fwg_ref, fwu_ref, fbg_ref, fbu_ref, wd_ref,
                       g_ref, b_ref, y_ref, st_ref, xb_ref, acc_ref, hg_ref, hu_ref, act_ref, cg_ref, cu_ref):
    i = pl.program_id(0)
    c = pl.program_id(1)
    tm = x_ref.shape[0]
    fc = wg_ref.shape[1]
    first = (i % tiles_per_seq) == 0
    hist = FFN_CONV_W - 1

    @pl.when(c == 0)
    def _():
        xb_ref[...] = x_ref[...].astype(BF16)

    for h_ref, carry_ref in ((hg_ref, cg_ref), (hu_ref, cu_ref)):
        @pl.when(first)
        def _():
            h_ref[0:SUBLANES, :] = jnp.zeros((SUBLANES, fc), F32)

        @pl.when(jnp.logical_not(first))
        def _():
            h_ref[0:SUBLANES, :] = carry_ref[c]

    def conv(h_ref, fw_ref, fb_ref, r0, cs):
        win = h_ref[r0:r0 + FFN_R + SUBLANES, cs]
        cur = slice(SUBLANES, SUBLANES + FFN_R)
        return (fw_ref[0:1, cs] * pltpu.roll(win, 2, 0)[cur, :]
                + fw_ref[1:2, cs] * pltpu.roll(win, 1, 0)[cur, :]
                + fw_ref[2:3, cs] * win[cur, :] + fb_ref[:, cs])

    sr = tm // FFN_SUB
    downs = []
    for sub in range(FFN_SUB):
        rs = slice(sub * sr, (sub + 1) * sr)
        hs = slice(SUBLANES + sub * sr, SUBLANES + (sub + 1) * sr)
        xs = xb_ref[rs, :]
        hg_ref[hs, :] = jnp.dot(xs, wg_ref[...], preferred_element_type=F32)
        hu_ref[hs, :] = jnp.dot(xs, wu_ref[...], preferred_element_type=F32)
        for r0 in range(sub * sr, (sub + 1) * sr, FFN_R):
            for cc in range(fc // FFN_C):
                cs = slice(cc * FFN_C, (cc + 1) * FFN_C)
                gt = conv(hg_ref, fwg_ref, fbg_ref, r0, cs)
                u = conv(hu_ref, fwu_ref, fbu_ref, r0, cs)
                act_ref[r0:r0 + FFN_R, cs] = (_silu(gt) * u).astype(BF16)
        downs.append(jnp.dot(act_ref[rs, :], wd_ref[...], preferred_element_type=F32))
    down = jnp.concatenate(downs, axis=0)

    cg_ref[c] = hg_ref[tm:tm + SUBLANES, :]
    cu_ref[c] = hu_ref[tm:tm + SUBLANES, :]
    for r in range(hist):
        row = SUBLANES + tm - hist + r
        st_ref[r] = jnp.concatenate([hg_ref[row:row + 1, :], hu_ref[row:row + 1, :]], axis=0)
    last = pl.num_programs(1) - 1

    @pl.when(jnp.logical_and(c == 0, c != last))
    def _():
        acc_ref[...] = down

    @pl.when(jnp.logical_and(c > 0, c != last))
    def _():
        acc_ref[...] += down

    @pl.when(jnp.logical_and(c == last, c > 0))
    def _():
        y_ref[...] = _layer_norm(ALPHA * x_ref[...] + (acc_ref[...] + down), g_ref[...], b_ref[...])

    @pl.when(jnp.logical_and(c == last, c == 0))
    def _():
        y_ref[...] = _layer_norm(ALPHA * x_ref[...] + down, g_ref[...], b_ref[...])


def _ffn_prompt(x, w_up, w_fdw, b_fdw, w_down, g, b, nb_batch, seq):
    m, d = x.shape
    f = w_down.shape[0]
    tm = TM_FFN
    fc = FC_FFN if f % FC_FFN == 0 else FFN_C
    assert seq % tm == 0 and f % fc == 0 and FFN_CONV_W == 3 and tm % FFN_R == 0 and fc % FFN_C == 0
    nc = f // fc
    tps = seq // tm
    vec = pl.BlockSpec((1, d), lambda i, c: (0, 0))
    y, st = pl.pallas_call(
        functools.partial(_ffn_prompt_kernel, tps),
        grid=(m // tm, nc),
        in_specs=[pl.BlockSpec((tm, d), lambda i, c: (i, 0)),
                  pl.BlockSpec((d, fc), lambda i, c: (0, c)), pl.BlockSpec((d, fc), lambda i, c: (0, nc + c)),
                  pl.BlockSpec((FFN_CONV_W, fc), lambda i, c: (0, c)),
                  pl.BlockSpec((FFN_CONV_W, fc), lambda i, c: (0, nc + c)),
                  pl.BlockSpec((1, fc), lambda i, c: (0, c)), pl.BlockSpec((1, fc), lambda i, c: (0, nc + c)),
                  pl.BlockSpec((fc, d), lambda i, c: (c, 0)), vec, vec],
        out_specs=[pl.BlockSpec((tm, d), lambda i, c: (i, 0)),
                   pl.BlockSpec((None, FFN_CONV_W - 1, 2, fc), lambda i, c: (i, 0, 0, c))],
        out_shape=[jax.ShapeDtypeStruct((m, d), F32),
                   jax.ShapeDtypeStruct((m // tm, FFN_CONV_W - 1, 2, f), F32)],
        scratch_shapes=[pltpu.VMEM((tm, d), BF16), pltpu.VMEM((tm, d), F32),
                        pltpu.VMEM((tm + SUBLANES, fc), F32), pltpu.VMEM((tm + SUBLANES, fc), F32),
                        pltpu.VMEM((tm, fc), BF16),
                        pltpu.VMEM((nc, SUBLANES, fc), F32), pltpu.VMEM((nc, SUBLANES, fc), F32)],
        compiler_params=_params("arbitrary", "arbitrary"),
        name="ffn_prompt",
    )(x, w_up, w_up, w_fdw, w_fdw, b_fdw.reshape(1, 2 * f), b_fdw.reshape(1, 2 * f), w_down,
      g.reshape(1, d), b.reshape(1, d))
    return y, st[tps - 1::tps].reshape(nb_batch, FFN_CONV_W - 1, 2 * f)


def _ffn_sample_kernel(x_ref, s0g_ref, s0u_ref, s1g_ref, s1u_ref, wg_ref, wu_ref, fwg_ref, fwu_ref,
                       fbg_ref, fbu_ref, wd_ref, g_ref, b_ref, y_ref, hg_ref, hu_ref, acc_ref):
    c = pl.program_id(0)

    @pl.when(c == 0)
    def _():
        acc_ref[...] = jnp.zeros_like(acc_ref)

    xb = x_ref[...].astype(BF16)
    hg = jnp.dot(xb, wg_ref[...], preferred_element_type=F32)
    hu = jnp.dot(xb, wu_ref[...], preferred_element_type=F32)
    hg_ref[...] = hg
    hu_ref[...] = hu
    gt = fwg_ref[0:1, :] * s0g_ref[...] + fwg_ref[1:2, :] * s1g_ref[...] + fwg_ref[2:3, :] * hg + fbg_ref[...]
    u = fwu_ref[0:1, :] * s0u_ref[...] + fwu_ref[1:2, :] * s1u_ref[...] + fwu_ref[2:3, :] * hu + fbu_ref[...]
    act = (_silu(gt) * u).astype(BF16)
    acc_ref[...] += jnp.dot(act, wd_ref[...], preferred_element_type=F32)

    @pl.when(c == pl.num_programs(0) - 1)
    def _():
        y_ref[...] = _layer_norm(ALPHA * x_ref[...] + acc_ref[...], g_ref[...], b_ref[...])


def _ffn_sample(x, state, w_up, w_fdw, b_fdw, w_down, g, b):
    nb, d = x.shape
    f = w_down.shape[0]
    fc = FC_FFN_S
    assert f % fc == 0 and FFN_CONV_W == 3 and state.shape == (nb, FFN_CONV_W - 1, 2 * f)
    nc = f // fc
    st = state.reshape(nb, 4 * f)
    vec = pl.BlockSpec((1, d), lambda c: (0, 0))
    col = lambda k: pl.BlockSpec((nb, fc), lambda c: (0, k * nc + c))
    y, hg, hu = pl.pallas_call(
        _ffn_sample_kernel,
        grid=(nc,),
        in_specs=[pl.BlockSpec((nb, d), lambda c: (0, 0)), col(0), col(1), col(2), col(3),
                  pl.BlockSpec((d, fc), lambda c: (0, c)), pl.BlockSpec((d, fc), lambda c: (0, nc + c)),
                  pl.BlockSpec((FFN_CONV_W, fc), lambda c: (0, c)),
                  pl.BlockSpec((FFN_CONV_W, fc), lambda c: (0, nc + c)),
                  pl.BlockSpec((1, fc), lambda c: (0, c)), pl.BlockSpec((1, fc), lambda c: (0, nc + c)),
                  pl.BlockSpec((fc, d), lambda c: (c, 0)), vec, vec],
        out_specs=[pl.BlockSpec((nb, d), lambda c: (0, 0)), pl.BlockSpec((nb, fc), lambda c: (0, c)),
                   pl.BlockSpec((nb, fc), lambda c: (0, c))],
        out_shape=[jax.ShapeDtypeStruct((nb, d), F32), jax.ShapeDtypeStruct((nb, f), F32),
                   jax.ShapeDtypeStruct((nb, f), F32)],
        scratch_shapes=[pltpu.VMEM((nb, d), F32)],
        compiler_params=_params("arbitrary"),
        name="ffn_sample",
    )(x, st, st, st, st, w_up, w_up, w_fdw, w_fdw, b_fdw.reshape(1, 2 * f), b_fdw.reshape(1, 2 * f), w_down,
      g.reshape(1, d), b.reshape(1, d))
    new_state = jnp.stack([state[:, 1, :], jnp.concatenate([hg, hu], axis=1)], axis=1)
    return y, new_state


def kernel(x_prompt, x_sample, cache_k, cache_v, state_conv, state_ffn, page_table, w_qkv, w_o, w_pw1, b_pw1,
           w_dw, b_dw, ln_cv_g, ln_cv_b, w_pw2, b_pw2, w_up, w_fdw, b_fdw, w_down, ln1_g, ln1_b, ln2_g, ln2_b):
    bp, sp, d = x_prompt.shape
    bs, ss, _ = x_sample.shape
    assert d == D_MODEL and ss == 1
    slopes = jnp.asarray(_alibi_slopes())
    yp = x_prompt.reshape(bp * sp, d)
    ys = x_sample.reshape(bs, d)
    k_pr, v_pr, k_sa, v_sa, c_pr, c_sa, f_pr, f_sa = [], [], [], [], [], [], [], []
    for i in range(DEPTH):
        j = i // 2
        if i % 2 == 0:
            wq = w_qkv[j].astype(BF16)
            wo = w_o[j].astype(BF16)
            q, qs, k, v, kf, vf, km = _qkv_prompt(yp, wq, bp, sp)
            op = _moba_prompt(q, qs, k, v, km.reshape(bp, sp // BLOCK, d), slopes)
            qkv_s = _mm(ys, wq)
            q_s, k_s, v_s = qkv_s[:, 0:d], qkv_s[:, d:2 * d], qkv_s[:, 2 * d:3 * d]
            os_ = _moba_sample(q_s, k_s, v_s, cache_k, cache_v, page_table, j, slopes)
            yp = _proj_ln(op, wo, None, yp, ln1_g[i], ln1_b[i])
            ys = _proj_ln(os_, wo, None, ys, ln1_g[i], ln1_b[i])
            k_pr.append(kf.reshape(bp, sp, N_HEADS, HEAD_DIM))
            v_pr.append(vf.reshape(bp, sp, N_HEADS, HEAD_DIM))
            k_sa.append(k_s.reshape(bs, ss, N_HEADS, HEAD_DIM))
            v_sa.append(v_s.reshape(bs, ss, N_HEADS, HEAD_DIM))
        else:
            w1 = w_pw1[j].astype(BF16)
            w2 = w_pw2[j].astype(BF16)
            up = _pw1_glu(yp, w1, b_pw1[j])
            zp = _dwconv_ln_prompt(up, w_dw[j], b_dw[j], ln_cv_g[j], ln_cv_b[j], sp)
            us = _pw1_glu(ys, w1, b_pw1[j])
            zs = _dwconv_ln_sample(state_conv[j], us, w_dw[j], b_dw[j], ln_cv_g[j], ln_cv_b[j])
            yp = _proj_ln(zp, w2, b_pw2[j], yp, ln1_g[i], ln1_b[i])
            ys = _proj_ln(zs, w2, b_pw2[j], ys, ln1_g[i], ln1_b[i])
            c_pr.append(up.reshape(bp, sp, d)[:, sp - (CONV_W - 1):, :])
            c_sa.append(jnp.concatenate([state_conv[j][:, 1:, :], us[:, None, :]], axis=1))
        wu = w_up[i].astype(BF16)
        wd = w_down[i].astype(BF16)
        yp, fp = _ffn_prompt(yp, wu, w_fdw[i], b_fdw[i], wd, ln2_g[i], ln2_b[i], bp, sp)
        ys, fs = _ffn_sample(ys, state_ffn[i], wu, w_fdw[i], b_fdw[i], wd, ln2_g[i], ln2_b[i])
        f_pr.append(fp)
        f_sa.append(fs)
    return (yp.reshape(bp, sp, d), ys.reshape(bs, ss, d), jnp.stack(k_pr), jnp.stack(v_pr), jnp.stack(k_sa),
            jnp.stack(v_sa), jnp.stack(c_pr), jnp.stack(c_sa), jnp.stack(f_pr), jnp.stack(f_sa))
```

```python
import functools
import math

import numpy as np
import jax
import jax.numpy as jnp
from jax import lax
from jax.experimental import pallas as pl
from jax.experimental.pallas import tpu as pltpu

N_HEADS = 8
HEAD_DIM = 128
D_MODEL = N_HEADS * HEAD_DIM
BLOCK = 256
TOPK = 3
CONV_W = 31
FFN_CONV_W = 3
DEPTH = 4
ALPHA = (2.0 * DEPTH) ** 0.25
LN_EPS = 1e-5
NEG = -1e30
SCALE = 1.0 / math.sqrt(HEAD_DIM)

LANES = 128
SUBLANES = 8
VMEM_LIMIT = 56 * 1024 * 1024

F32 = jnp.float32
BF16 = jnp.bfloat16

TM_ROWS = 512
TM_CONV = 256
CONV_HALO = 32
CONV_R = 64
CONV_C = 256
MOBA_QT = 1024
MOBA_KT = 1024
TM_FFN = 512
FC_FFN = 1408
FFN_SUB = 2
FFN_R = 32
FFN_C = 128
FC_FFN_S = 256
TB_CONV_S = 32

_NT = (((1,), (1,)), ((), ()))


def _alibi_slopes():
    s = np.exp2(-8.0 * (np.arange(N_HEADS, dtype=np.float64) + 1.0) / N_HEADS).astype(np.float32)
    assert np.all(s.astype(jnp.bfloat16).astype(np.float32) == s)
    assert np.all(np.log2(s) == np.round(np.log2(s)))
    return s


def _params(*sem):
    return pltpu.CompilerParams(dimension_semantics=sem, vmem_limit_bytes=VMEM_LIMIT)


def _layer_norm(x, g, b):
    mu = jnp.mean(x, axis=-1, keepdims=True)
    xc = x - mu
    var = jnp.mean(xc * xc, axis=-1, keepdims=True)
    return xc * lax.rsqrt(var + LN_EPS) * g + b


def _silu(x):
    return x * jax.nn.sigmoid(x)


def _qkv_prompt_kernel(n_aliased, x_ref, w_ref, *refs):
    q_ref, qs_ref, k_ref, v_ref, kf_ref, vf_ref, km_ref = refs[n_aliased:]
    tm = x_ref.shape[0]
    xb = x_ref[...].astype(BF16)
    q = jnp.dot(xb, w_ref[:, 0:D_MODEL], preferred_element_type=F32)
    qs = q * SCALE
    for h in range(N_HEADS):
        sl = slice(h * HEAD_DIM, (h + 1) * HEAD_DIM)
        q_ref[h] = q[:, sl].astype(BF16)
        qs_ref[h] = qs[:, sl].astype(BF16)
    k = jnp.dot(xb, w_ref[:, D_MODEL:2 * D_MODEL], preferred_element_type=F32)
    kf_ref[...] = pltpu.einshape("m(hd)->mhd", k, h=N_HEADS)
    for h in range(N_HEADS):
        k_ref[h] = k[:, h * HEAD_DIM:(h + 1) * HEAD_DIM].astype(BF16)
    for j in range(tm // BLOCK):
        km_ref[j] = jnp.mean(k[j * BLOCK:(j + 1) * BLOCK], axis=0, keepdims=True)
    v = jnp.dot(xb, w_ref[:, 2 * D_MODEL:3 * D_MODEL], preferred_element_type=F32)
    vf_ref[...] = pltpu.einshape("m(hd)->mhd", v, h=N_HEADS)
    ones = jnp.ones((tm, HEAD_DIM), BF16)
    for h in range(N_HEADS):
        v_ref[h] = jnp.concatenate([v[:, h * HEAD_DIM:(h + 1) * HEAD_DIM].astype(BF16), ones], axis=1)


def _qkv_prompt(x, w, nb_batch, seq, layer, kv_prev):
    m, d = x.shape
    tm = TM_ROWS
    assert seq % tm == 0 and tm % BLOCK == 0 and d == D_MODEL
    nt = seq // tm
    n_attn = (DEPTH + 1) // 2
    hm = jax.ShapeDtypeStruct((nb_batch, N_HEADS, seq, HEAD_DIM), BF16)
    hm_spec = pl.BlockSpec((None, N_HEADS, tm, HEAD_DIM), lambda b, i: (b, 0, i, 0))
    hm2 = jax.ShapeDtypeStruct((nb_batch, N_HEADS, seq, 2 * HEAD_DIM), BF16)
    hm2_spec = pl.BlockSpec((None, N_HEADS, tm, 2 * HEAD_DIM), lambda b, i: (b, 0, i, 0))
    row_spec = pl.BlockSpec((tm, d), lambda b, i: (b * nt + i, 0))
    kv = jax.ShapeDtypeStruct((n_attn, m, N_HEADS, HEAD_DIM), F32)
    kv_spec = pl.BlockSpec((None, tm, N_HEADS, HEAD_DIM), lambda b, i: (layer, b * nt + i, 0, 0))
    in_specs = [row_spec, pl.BlockSpec((d, 3 * d), lambda b, i: (0, 0))]
    args = [x, w]
    aliases = {}
    if kv_prev is not None:
        in_specs += [pl.BlockSpec(memory_space=pl.ANY)] * 2
        args += list(kv_prev)
        aliases = {2: 4, 3: 5}
    return pl.pallas_call(
        functools.partial(_qkv_prompt_kernel, len(aliases)),
        grid=(nb_batch, nt),
        in_specs=in_specs,
        out_specs=[hm_spec, hm_spec, hm_spec, hm2_spec, kv_spec, kv_spec,
                   pl.BlockSpec((tm // BLOCK, 1, d), lambda b, i: (b * nt + i, 0, 0))],
        out_shape=[hm, hm, hm, hm2, kv, kv, jax.ShapeDtypeStruct((m // BLOCK, 1, d), F32)],
        input_output_aliases=aliases,
        compiler_params=_params("parallel", "parallel"),
        name="qkv_prompt",
    )(*args)


def _mm_kernel(x_ref, w_ref, o_ref):
    o_ref[...] = jnp.dot(x_ref[...].astype(BF16), w_ref[...], preferred_element_type=F32)


def _mm(x, w, tn=1024):
    m, k = x.shape
    n = w.shape[1]
    assert n % tn == 0
    return pl.pallas_call(
        _mm_kernel,
        grid=(n // tn,),
        in_specs=[pl.BlockSpec((m, k), lambda j: (0, 0)), pl.BlockSpec((k, tn), lambda j: (0, j))],
        out_specs=pl.BlockSpec((m, tn), lambda j: (0, j)),
        out_shape=jax.ShapeDtypeStruct((m, n), F32),
        compiler_params=_params("parallel"),
        name="mm_rows",
    )(x, w)


def _proj_ln_kernel(has_bias, a_ref, w_ref, *refs):
    if has_bias:
        bias_ref, y_ref, g_ref, b_ref, o_ref = refs
    else:
        y_ref, g_ref, b_ref, o_ref = refs
    mp = jnp.dot(a_ref[...].astype(BF16), w_ref[...], preferred_element_type=F32)
    if has_bias:
        mp = mp + bias_ref[...]
    o_ref[...] = _layer_norm(ALPHA * y_ref[...] + mp, g_ref[...], b_ref[...])


def _proj_ln(a, w, bias, y, g, b):
    m, d = y.shape
    tm = min(TM_ROWS, m)
    assert m % tm == 0
    row = lambda i: (i, 0)
    fixed = lambda i: (0, 0)
    vec = pl.BlockSpec((1, d), fixed)
    in_specs = [pl.BlockSpec((tm, a.shape[1]), row), pl.BlockSpec(w.shape, fixed)]
    args = [a, w]
    if bias is not None:
        in_specs.append(vec)
        args.append(bias.reshape(1, d))
    in_specs += [pl.BlockSpec((tm, d), row), vec, vec]
    args += [y, g.reshape(1, d), b.reshape(1, d)]
    return pl.pallas_call(
        functools.partial(_proj_ln_kernel, bias is not None),
        grid=(m // tm,),
        in_specs=in_specs,
        out_specs=pl.BlockSpec((tm, d), row),
        out_shape=jax.ShapeDtypeStruct((m, d), F32),
        compiler_params=_params("parallel"),
        name="proj_ln",
    )(*args)


def _pw1_glu_kernel(x_ref, w_ref, b_ref, u_ref):
    d = u_ref.shape[1]
    xb = x_ref[...].astype(BF16)
    a = jnp.dot(xb, w_ref[:, 0:d], preferred_element_type=F32) + b_ref[:, 0:d]
    gt = jnp.dot(xb, w_ref[:, d:2 * d], preferred_element_type=F32) + b_ref[:, d:2 * d]
    u_ref[...] = a * jax.nn.sigmoid(gt)


def _pw1_glu(x, w, bias):
    m, d = x.shape
    tm = min(TM_ROWS, m)
    assert m % tm == 0
    return pl.pallas_call(
        _pw1_glu_kernel,
        grid=(m // tm,),
        in_specs=[pl.BlockSpec((tm, d), lambda i: (i, 0)), pl.BlockSpec((d, 2 * d), lambda i: (0, 0)),
                  pl.BlockSpec((1, 2 * d), lambda i: (0, 0))],
        out_specs=pl.BlockSpec((tm, d), lambda i: (i, 0)),
        out_shape=jax.ShapeDtypeStruct((m, d), F32),
        compiler_params=_params("parallel"),
        name="pw1_glu",
    )(x, w, bias.reshape(1, 2 * d))


def _dwconv_ln_prompt_kernel(tiles_per_seq, u_ref, halo_ref, w_ref, b_ref, g_ref, be_ref, o_ref, ext_ref, y_ref):
    i = pl.program_id(0)
    tm, d = u_ref.shape
    first = (i % tiles_per_seq) == 0
    ext_ref[0:CONV_HALO, :] = jnp.where(first, 0.0, halo_ref[...])
    ext_ref[CONV_HALO:CONV_HALO + tm, :] = u_ref[...]
    off = CONV_HALO - (CONV_W - 1)

    def row_body(r, carry):
        r0 = pl.multiple_of(r * CONV_R, CONV_R)
        for c in range(d // CONV_C):
            cs = slice(c * CONV_C, (c + 1) * CONV_C)
            win = ext_ref[pl.ds(r0, CONV_R + CONV_HALO), cs]
            acc = jnp.zeros((CONV_R, CONV_C), F32)
            for ph in range(SUBLANES):
                taps = [k for k in range(CONV_W) if (off + k) % SUBLANES == ph]
                if not taps:
                    continue
                wp = win if ph == 0 else pltpu.roll(win, CONV_R + CONV_HALO - ph, 0)
                for k in taps:
                    a = off + k - ph
                    acc = acc + wp[a:a + CONV_R, :] * w_ref[k:k + 1, cs]
            y_ref[pl.ds(r0, CONV_R), cs] = acc + b_ref[:, cs]
        return carry

    lax.fori_loop(0, tm // CONV_R, row_body, 0)
    z = _layer_norm(y_ref[...], g_ref[...], be_ref[...])
    o_ref[...] = _silu(z).astype(BF16)


def _dwconv_ln_prompt(u, w, bias, g, b, seq):
    m, d = u.shape
    tm = TM_CONV
    assert seq % tm == 0 and tm % CONV_R == 0 and tm % CONV_HALO == 0 and d % CONV_C == 0
    assert CONV_HALO >= CONV_W - 1 and CONV_HALO % SUBLANES == 0
    hb = tm // CONV_HALO
    vec = pl.BlockSpec((1, d), lambda i: (0, 0))
    return pl.pallas_call(
        functools.partial(_dwconv_ln_prompt_kernel, seq // tm),
        grid=(m // tm,),
        in_specs=[pl.BlockSpec((tm, d), lambda i: (i, 0)),
                  pl.BlockSpec((CONV_HALO, d), lambda i: (jnp.maximum(i * hb - 1, 0), 0)),
                  pl.BlockSpec((CONV_W, d), lambda i: (0, 0)), vec, vec, vec],
        out_specs=pl.BlockSpec((tm, d), lambda i: (i, 0)),
        out_shape=jax.ShapeDtypeStruct((m, d), BF16),
        scratch_shapes=[pltpu.VMEM((tm + CONV_HALO, d), F32), pltpu.VMEM((tm, d), F32)],
        compiler_params=_params("parallel"),
        name="dwconv_ln_prompt",
    )(u, u, w, bias.reshape(1, d), g.reshape(1, d), b.reshape(1, d))


def _dwconv_ln_sample_kernel(st_ref, u_ref, w_ref, b_ref, g_ref, be_ref, o_ref):
    st = st_ref[...]
    w = w_ref[...]
    y = jnp.sum(st * w[None, 0:CONV_W - 1, :], axis=1)
    y = y + u_ref[...] * w[CONV_W - 1:CONV_W, :] + b_ref[...]
    z = _layer_norm(y, g_ref[...], be_ref[...])
    o_ref[...] = _silu(z).astype(BF16)


def _dwconv_ln_sample(state, u, w, bias, g, b):
    nb, hist, d = state.shape
    tb = min(TB_CONV_S, nb)
    assert nb % tb == 0 and hist == CONV_W - 1
    vec = pl.BlockSpec((1, d), lambda i: (0, 0))
    return pl.pallas_call(
        _dwconv_ln_sample_kernel,
        grid=(nb // tb,),
        in_specs=[pl.BlockSpec((tb, hist, d), lambda i: (i, 0, 0)), pl.BlockSpec((tb, d), lambda i: (i, 0)),
                  pl.BlockSpec((CONV_W, d), lambda i: (0, 0)), vec, vec, vec],
        out_specs=pl.BlockSpec((tb, d), lambda i: (i, 0)),
        out_shape=jax.ShapeDtypeStruct((nb, d), BF16),
        compiler_params=_params("parallel"),
        name="dwconv_ln_sample",
    )(state, u, w, bias.reshape(1, d), g.reshape(1, d), b.reshape(1, d))


def _aug_lanes(nb):
    return dict(tq=nb, r=nb + 1, kb=nb + 2, qb=nb + 3)


def _moba_prompt_kernel(slopes_ref, q_ref, qs_ref, k_ref, v_ref, km_ref, o_ref, tab_ref, acc_ref):
    h = pl.program_id(1)
    i = pl.program_id(2)
    qt = q_ref.shape[0]
    nsub = qt // BLOCK
    nb = k_ref.shape[0] // BLOCK
    ln = _aug_lanes(nb)
    slope = slopes_ref[h]
    lane = lax.broadcasted_iota(jnp.int32, (qt, HEAD_DIM), 1)
    rowi = lax.broadcasted_iota(jnp.int32, (qt, HEAD_DIM), 0)

    @pl.when(i == 0)
    def _build_key_table():
        lane_b = lax.broadcasted_iota(jnp.int32, (BLOCK, HEAD_DIM), 1)
        rowf_b = lax.broadcasted_iota(jnp.int32, (BLOCK, HEAD_DIM), 0).astype(F32)
        base = jnp.where((lane_b == ln["tq"]) | (lane_b == ln["qb"]), 1.0, 0.0)
        base = jnp.where(lane_b == ln["r"], slope * rowf_b, base)

        def tab_body(b, carry):
            t = jnp.where(lane_b == b, 1.0, base)
            t = jnp.where(lane_b == ln["kb"], slope * jnp.asarray(BLOCK * b, F32), t)
            tab_ref[pl.ds(pl.multiple_of(b * BLOCK, BLOCK), BLOCK), :] = t.astype(BF16)
            return carry

        lax.fori_loop(0, nb, tab_body, 0)

    own = nsub * i + rowi // BLOCK
    km = jnp.concatenate([km_ref[...].astype(BF16), jnp.zeros((HEAD_DIM - nb, HEAD_DIM), BF16)], axis=0)
    gate = lax.dot_general(q_ref[...], km, _NT, preferred_element_type=F32)
    past = lane < own
    gw = jnp.where(past, gate, -jnp.inf)
    valid_bias = jnp.where(past, 0.0, NEG)
    selbias = jnp.full((qt, HEAD_DIM), NEG, F32)
    lanef = lane.astype(F32)
    for _ in range(TOPK):
        mx = jnp.max(gw, axis=1, keepdims=True)
        idx = jnp.min(jnp.where(gw == mx, lanef, float(HEAD_DIM)), axis=1, keepdims=True)
        pick = lanef == idx
        selbias = jnp.where(pick, valid_bias, selbias)
        gw = jnp.where(pick, -jnp.inf, gw)
    selbias = jnp.where(lane == own, 0.0, selbias)

    aug = jnp.where(lane < nb, selbias, 0.0)
    aug = jnp.where(lane == ln["tq"], -slope * (rowi % BLOCK).astype(F32), aug)
    aug = jnp.where((lane == ln["r"]) | (lane == ln["kb"]), 1.0, aug)
    aug = jnp.where(lane == ln["qb"], -slope * (BLOCK * own).astype(F32), aug)
    qa = jnp.concatenate([qs_ref[...], aug.astype(BF16)], axis=1)
    qas = [qa[a * BLOCK:(a + 1) * BLOCK, :] for a in range(nsub)]

    def key_tile(j):
        rows = pl.ds(pl.multiple_of(j * MOBA_KT, MOBA_KT), MOBA_KT)
        return jnp.concatenate([k_ref[rows, :], tab_ref[rows, :]], axis=1), v_ref[rows, :]

    ka, vb = key_tile(i)
    ms = []
    for a in range(nsub):
        nk = (a + 1) * BLOCK
        s = lax.dot_general(qas[a], ka[0:nk, :], _NT, preferred_element_type=F32)
        diff = (lax.broadcasted_iota(jnp.int32, (BLOCK, nk), 1)
                - lax.broadcasted_iota(jnp.int32, (BLOCK, nk), 0))
        s = jnp.where(diff <= a * BLOCK, s, NEG)
        m0 = jnp.max(s, axis=1, keepdims=True)
        ms.append(m0)
        acc_ref[a] = jnp.dot(jnp.exp(s - m0).astype(BF16), vb[0:nk, :], preferred_element_type=F32)

    def tile_body(j, ms):
        ka, vb = key_tile(j)
        out = []
        for a in range(nsub):
            s = lax.dot_general(qas[a], ka, _NT, preferred_element_type=F32)
            m_new = jnp.maximum(ms[a], jnp.max(s, axis=1, keepdims=True))
            out.append(m_new)
            acc_ref[a] = (jnp.exp(ms[a] - m_new) * acc_ref[a]
                          + jnp.dot(jnp.exp(s - m_new).astype(BF16), vb, preferred_element_type=F32))
        return tuple(out)

    lax.fori_loop(0, i, tile_body, tuple(ms))
    for a in range(nsub):
        acc = acc_ref[a]
        o_ref[a * BLOCK:(a + 1) * BLOCK, :] = (acc[:, 0:HEAD_DIM] / acc[:, HEAD_DIM:2 * HEAD_DIM]).astype(BF16)


def _moba_prompt(q, qs, k, v, km, slopes):
    nbt, nh, seq, hd = q.shape
    qt = MOBA_QT
    assert seq % qt == 0 and MOBA_KT == qt and qt % BLOCK == 0
    assert hd == HEAD_DIM == LANES and nh == N_HEADS and v.shape[-1] == 2 * hd
    nq = seq // qt
    nblk = seq // BLOCK
    assert nblk + 4 <= HEAD_DIM
    qspec = pl.BlockSpec((None, None, qt, hd), lambda b, h, i: (b, h, i, 0))
    kspec = pl.BlockSpec((None, None, seq, hd), lambda b, h, i: (b, h, 0, 0))
    vspec = pl.BlockSpec((None, None, seq, 2 * hd), lambda b, h, i: (b, h, 0, 0))
    return pl.pallas_call(
        _moba_prompt_kernel,
        grid=(nbt, nh, nq),
        in_specs=[pl.BlockSpec(memory_space=pltpu.SMEM), qspec, qspec, kspec, vspec,
                  pl.BlockSpec((None, nblk, hd), lambda b, h, i: (b, 0, h))],
        out_specs=pl.BlockSpec((qt, hd), lambda b, h, i: (b * nq + i, h)),
        out_shape=jax.ShapeDtypeStruct((nbt * seq, nh * hd), BF16),
        scratch_shapes=[pltpu.VMEM((seq, hd), BF16), pltpu.VMEM((qt // BLOCK, BLOCK, 2 * hd), F32)],
        compiler_params=_params("parallel", "parallel", "arbitrary"),
        name="moba_prompt",
    )(slopes, q, qs, k, v, km)


def _moba_sample_kernel(n_pages, pt_ref, slopes_ref, q_ref, kn_ref, vn_ref, *refs):
    del pt_ref
    k_refs = refs[:n_pages]
    v_refs = refs[n_pages:2 * n_pages]
    o_ref = refs[2 * n_pages]
    base_ref = refs[2 * n_pages + 1]
    page = k_refs[0].shape[0]
    past_len = n_pages * page
    nbp = past_len // BLOCK
    ppb = BLOCK // page
    lpp = page * N_HEADS
    lpb = BLOCK * N_HEADS
    lane_b = lax.broadcasted_iota(jnp.int32, (N_HEADS, lpb), 1)
    row_b = lax.broadcasted_iota(jnp.int32, (N_HEADS, lpb), 0)
    own_head = (lane_b % N_HEADS) == row_b

    @pl.when(pl.program_id(0) == 0)
    def _build_bias():
        slopes = slopes_ref[...]
        for b in range(nbp):
            dist = (past_len - b * BLOCK - lane_b // N_HEADS).astype(F32)
            base_ref[:, b * lpb:(b + 1) * lpb] = jnp.where(own_head, -slopes * dist, NEG)

    q8 = q_ref[...]
    qb = q8.astype(BF16)

    raws = []
    for b in range(nbp):
        parts = []
        for pg in range(b * ppb, (b + 1) * ppb):
            kf = k_refs[pg][...].reshape(lpp, HEAD_DIM).astype(BF16)
            parts.append(lax.dot_general(qb, kf, _NT, preferred_element_type=F32))
        raws.append(jnp.concatenate(parts, axis=1))

    lane = lax.broadcasted_iota(jnp.int32, (N_HEADS, LANES), 1)
    gw = jnp.full((N_HEADS, LANES), -jnp.inf, F32)
    for b in range(nbp):
        gb = jnp.sum(jnp.where(own_head, raws[b], 0.0), axis=1, keepdims=True) * (1.0 / BLOCK)
        gw = jnp.where(lane == b, gb, gw)
    sel = jnp.zeros((N_HEADS, LANES), F32)
    lanef = lane.astype(F32)
    for _ in range(min(TOPK, nbp)):
        mx = jnp.max(gw, axis=1, keepdims=True)
        idx = jnp.min(jnp.where(gw == mx, lanef, float(LANES)), axis=1, keepdims=True)
        pick = lanef == idx
        sel = jnp.where(pick, 1.0, sel)
        gw = jnp.where(pick, -jnp.inf, gw)

    scores = [jnp.where(sel[:, b:b + 1] > 0.5, raws[b] * SCALE + base_ref[:, b * lpb:(b + 1) * lpb], NEG)
              for b in range(nbp)]
    s_new = jnp.sum(q8 * kn_ref[...], axis=1, keepdims=True) * SCALE

    m = s_new
    for b in range(nbp):
        m = jnp.maximum(m, jnp.max(scores[b], axis=1, keepdims=True))
    p_new = jnp.exp(s_new - m)
    l = p_new
    acc = p_new * vn_ref[...]
    for b in range(nbp):
        p = jnp.exp(scores[b] - m)
        l = l + jnp.sum(p, axis=1, keepdims=True)
        pb = p.astype(BF16)
        for t in range(ppb):
            vf = v_refs[b * ppb + t][...].reshape(lpp, HEAD_DIM).astype(BF16)
            acc = acc + jnp.dot(pb[:, t * lpp:(t + 1) * lpp], vf, preferred_element_type=F32)
    o_ref[...] = acc / l


def _moba_sample(q, k_new, v_new, cache_k, cache_v, page_table, layer, slopes):
    nb, d = q.shape
    n_pages = page_table.shape[1]
    page = cache_k.shape[2]
    assert BLOCK % page == 0 and (n_pages * page) % BLOCK == 0 and d == D_MODEL
    assert cache_k.shape[3:] == (N_HEADS, HEAD_DIM) and N_HEADS == SUBLANES
    row = pl.BlockSpec((None, N_HEADS, HEAD_DIM), lambda b, pt: (b, 0, 0))

    def page_spec(p):
        return pl.BlockSpec((None, None, page, N_HEADS, HEAD_DIM), lambda b, pt: (layer, pt[b, p], 0, 0, 0))

    heads = lambda t: t.reshape(nb, N_HEADS, HEAD_DIM)
    out = pl.pallas_call(
        functools.partial(_moba_sample_kernel, n_pages),
        grid_spec=pltpu.PrefetchScalarGridSpec(
            num_scalar_prefetch=1,
            grid=(nb,),
            in_specs=[pl.BlockSpec((N_HEADS, 1), lambda b, pt: (0, 0)), row, row, row]
                     + [page_spec(p) for p in range(n_pages)] * 2,
            out_specs=row,
            scratch_shapes=[pltpu.VMEM((N_HEADS, n_pages * page * N_HEADS), F32)],
        ),
        out_shape=jax.ShapeDtypeStruct((nb, N_HEADS, HEAD_DIM), F32),
        compiler_params=_params("arbitrary"),
        name="moba_sample",
    )(page_table, slopes.reshape(N_HEADS, 1), heads(q), heads(k_new), heads(v_new),
      *([cache_k] * n_pages), *([cache_v] * n_pages))
    return out.reshape(nb, d)


def _ffn_prompt_kernel(tiles_per_seq, x_ref, wg_ref, wu_ref, fwg_ref, fwu_ref, fbg_ref, fbu_ref, wd_ref,
                       g_ref, b_ref, y_ref, st_ref, xb_ref, acc_ref, hg_ref, hu_ref, act_ref, cg_ref, cu_ref):
    i = pl.program_id(0)
    c = pl.program_id(1)
    tm = x_ref.shape[0]
    fc = wg_ref.shape[1]
    first = (i % tiles_per_seq) == 0
    hist = FFN_CONV_W - 1

    @pl.when(c == 0)
    def _():
        xb_ref[...] = x_ref[...].astype(BF16)

    for h_ref, carry_ref in ((hg_ref, cg_ref), (hu_ref, cu_ref)):
        @pl.when(first)
        def _():
            h_ref[0:SUBLANES, :] = jnp.zeros((SUBLANES, fc), F32)

        @pl.when(jnp.logical_not(first))
        def _():
            h_ref[0:SUBLANES, :] = carry_ref[c]

    def conv(h_ref, fw_ref, fb_ref, r0, cs):
        win = h_ref[r0:r0 + FFN_R + SUBLANES, cs]
        cur = slice(SUBLANES, SUBLANES + FFN_R)
        return (fw_ref[0:1, cs] * pltpu.roll(win, 2, 0)[cur, :]
                + fw_ref[1:2, cs] * pltpu.roll(win, 1, 0)[cur, :]
                + fw_ref[2:3, cs] * win[cur, :] + fb_ref[:, cs])

    sr = tm // FFN_SUB
    downs = []
    for sub in range(FFN_SUB):
        rs = slice(sub * sr, (sub + 1) * sr)
        hs = slice(SUBLANES + sub * sr, SUBLANES + (sub + 1) * sr)
        xs = xb_ref[rs, :]
        hg_ref[hs, :] = jnp.dot(xs, wg_ref[...], preferred_element_type=F32)
        hu_ref[hs, :] = jnp.dot(xs, wu_ref[...], preferred_element_type=F32)
        for r0 in range(sub * sr, (sub + 1) * sr, FFN_R):
            for cc in range(fc // FFN_C):
                cs = slice(cc * FFN_C, (cc + 1) * FFN_C)
                gt = conv(hg_ref, fwg_ref, fbg_ref, r0, cs)
                u = conv(hu_ref, fwu_ref, fbu_ref, r0, cs)
                act_ref[r0:r0 + FFN_R, cs] = (_silu(gt) * u).astype(BF16)
        downs.append(jnp.dot(act_ref[rs, :], wd_ref[...], preferred_element_type=F32))
    down = jnp.concatenate(downs, axis=0)

    cg_ref[c] = hg_ref[tm:tm + SUBLANES, :]
    cu_ref[c] = hu_ref[tm:tm + SUBLANES, :]
    for r in range(hist):
        row = SUBLANES + tm - hist + r
        st_ref[r] = jnp.concatenate([hg_ref[row:row + 1, :], hu_ref[row:row + 1, :]], axis=0)
    last = pl.num_programs(1) - 1

    @pl.when(jnp.logical_and(c == 0, c != last))
    def _():
        acc_ref[...] = down

    @pl.when(jnp.logical_and(c > 0, c != last))
    def _():
        acc_ref[...] += down

    @pl.when(jnp.logical_and(c == last, c > 0))
    def _():
        y_ref[...] = _layer_norm(ALPHA * x_ref[...] + (acc_ref[...] + down), g_ref[...], b_ref[...])

    @pl.when(jnp.logical_and(c == last, c == 0))
    def _():
        y_ref[...] = _layer_norm(ALPHA * x_ref[...] + down, g_ref[...], b_ref[...])


def _ffn_prompt(x, w_up, w_fdw, b_fdw, w_down, g, b, nb_batch, seq):
    m, d = x.shape
    f = w_down.shape[0]
    tm = TM_FFN
    fc = FC_FFN if f % FC_FFN == 0 else FFN_C
    assert seq % tm == 0 and f % fc == 0 and FFN_CONV_W == 3 and tm % FFN_R == 0 and fc % FFN_C == 0
    nc = f // fc
    tps = seq // tm
    vec = pl.BlockSpec((1, d), lambda i, c: (0, 0))
    y, st = pl.pallas_call(
        functools.partial(_ffn_prompt_kernel, tps),
        grid=(m // tm, nc),
        in_specs=[pl.BlockSpec((tm, d), lambda i, c: (i, 0)),
                  pl.BlockSpec((d, fc), lambda i, c: (0, c)), pl.BlockSpec((d, fc), lambda i, c: (0, nc + c)),
                  pl.BlockSpec((FFN_CONV_W, fc), lambda i, c: (0, c)),
                  pl.BlockSpec((FFN_CONV_W, fc), lambda i, c: (0, nc + c)),
                  pl.BlockSpec((1, fc), lambda i, c: (0, c)), pl.BlockSpec((1, fc), lambda i, c: (0, nc + c)),
                  pl.BlockSpec((fc, d), lambda i, c: (c, 0)), vec, vec],
        out_specs=[pl.BlockSpec((tm, d), lambda i, c: (i, 0)),
                   pl.BlockSpec((None, FFN_CONV_W - 1, 2, fc), lambda i, c: (i, 0, 0, c))],
        out_shape=[jax.ShapeDtypeStruct((m, d), F32),
                   jax.ShapeDtypeStruct((m // tm, FFN_CONV_W - 1, 2, f), F32)],
        scratch_shapes=[pltpu.VMEM((tm, d), BF16), pltpu.VMEM((tm, d), F32),
                        pltpu.VMEM((tm + SUBLANES, fc), F32), pltpu.VMEM((tm + SUBLANES, fc), F32),
                        pltpu.VMEM((tm, fc), BF16),
                        pltpu.VMEM((nc, SUBLANES, fc), F32), pltpu.VMEM((nc, SUBLANES, fc), F32)],
        compiler_params=_params("arbitrary", "arbitrary"),
        name="ffn_prompt",
    )(x, w_up, w_up, w_fdw, w_fdw, b_fdw.reshape(1, 2 * f), b_fdw.reshape(1, 2 * f), w_down,
      g.reshape(1, d), b.reshape(1, d))
    return y, st[tps - 1::tps].reshape(nb_batch, FFN_CONV_W - 1, 2 * f)


def _ffn_sample_kernel(x_ref, s0g_ref, s0u_ref, s1g_ref, s1u_ref, wg_ref, wu_ref, fwg_ref, fwu_ref,
                       fbg_ref, fbu_ref, wd_ref, g_ref, b_ref, y_ref, hg_ref, hu_ref, acc_ref):
    c = pl.program_id(0)

    @pl.when(c == 0)
    def _():
        acc_ref[...] = jnp.zeros_like(acc_ref)

    xb = x_ref[...].astype(BF16)
    hg = jnp.dot(xb, wg_ref[...], preferred_element_type=F32)
    hu = jnp.dot(xb, wu_ref[...], preferred_element_type=F32)
    hg_ref[...] = hg
    hu_ref[...] = hu
    gt = fwg_ref[0:1, :] * s0g_ref[...] + fwg_ref[1:2, :] * s1g_ref[...] + fwg_ref[2:3, :] * hg + fbg_ref[...]
    u = fwu_ref[0:1, :] * s0u_ref[...] + fwu_ref[1:2, :] * s1u_ref[...] + fwu_ref[2:3, :] * hu + fbu_ref[...]
    act = (_silu(gt) * u).astype(BF16)
    acc_ref[...] += jnp.dot(act, wd_ref[...], preferred_element_type=F32)

    @pl.when(c == pl.num_programs(0) - 1)
    def _():
        y_ref[...] = _layer_norm(ALPHA * x_ref[...] + acc_ref[...], g_ref[...], b_ref[...])


def _ffn_sample(x, state, w_up, w_fdw, b_fdw, w_down, g, b):
    nb, d = x.shape
    f = w_down.shape[0]
    fc = FC_FFN_S
    assert f % fc == 0 and FFN_CONV_W == 3 and state.shape == (nb, FFN_CONV_W - 1, 2 * f)
    nc = f // fc
    st = state.reshape(nb, 4 * f)
    vec = pl.BlockSpec((1, d), lambda c: (0, 0))
    col = lambda k: pl.BlockSpec((nb, fc), lambda c: (0, k * nc + c))
    y, hg, hu = pl.pallas_call(
        _ffn_sample_kernel,
        grid=(nc,),
        in_specs=[pl.BlockSpec((nb, d), lambda c: (0, 0)), col(0), col(1), col(2), col(3),
                  pl.BlockSpec((d, fc), lambda c: (0, c)), pl.BlockSpec((d, fc), lambda c: (0, nc + c)),
                  pl.BlockSpec((FFN_CONV_W, fc), lambda c: (0, c)),
                  pl.BlockSpec((FFN_CONV_W, fc), lambda c: (0, nc + c)),
                  pl.BlockSpec((1, fc), lambda c: (0, c)), pl.BlockSpec((1, fc), lambda c: (0, nc + c)),
                  pl.BlockSpec((fc, d), lambda c: (c, 0)), vec, vec],
        out_specs=[pl.BlockSpec((nb, d), lambda c: (0, 0)), pl.BlockSpec((nb, fc), lambda c: (0, c)),
                   pl.BlockSpec((nb, fc), lambda c: (0, c))],
        out_shape=[jax.ShapeDtypeStruct((nb, d), F32), jax.ShapeDtypeStruct((nb, f), F32),
                   jax.ShapeDtypeStruct((nb, f), F32)],
        scratch_shapes=[pltpu.VMEM((nb, d), F32)],
        compiler_params=_params("arbitrary"),
        name="ffn_sample",
    )(x, st, st, st, st, w_up, w_up, w_fdw, w_fdw, b_fdw.reshape(1, 2 * f), b_fdw.reshape(1, 2 * f), w_down,
      g.reshape(1, d), b.reshape(1, d))
    new_state = jnp.stack([state[:, 1, :], jnp.concatenate([hg, hu], axis=1)], axis=1)
    return y, new_state


def kernel(x_prompt, x_sample, cache_k, cache_v, state_conv, state_ffn, page_table, w_qkv, w_o, w_pw1, b_pw1,
           w_dw, b_dw, ln_cv_g, ln_cv_b, w_pw2, b_pw2, w_up, w_fdw, b_fdw, w_down, ln1_g, ln1_b, ln2_g, ln2_b):
    bp, sp, d = x_prompt.shape
    bs, ss, _ = x_sample.shape
    assert d == D_MODEL and ss == 1
    slopes = jnp.asarray(_alibi_slopes())
    yp = x_prompt.reshape(bp * sp, d)
    ys = x_sample.reshape(bs, d)
    k_sa, v_sa, c_pr, c_sa, f_pr, f_sa = [], [], [], [], [], []
    kv_pr = None
    kv_shape = ((DEPTH + 1) // 2, bp, sp, N_HEADS, HEAD_DIM)
    for i in range(DEPTH):
        j = i // 2
        if i % 2 == 0:
            wq = w_qkv[j].astype(BF16)
            wo = w_o[j].astype(BF16)
            q, qs, k, v, kf, vf, km = _qkv_prompt(yp, wq, bp, sp, j, kv_pr)
            kv_pr = (kf, vf)
            op = _moba_prompt(q, qs, k, v, km.reshape(bp, sp // BLOCK, d), slopes)
            qkv_s = _mm(ys, wq)
            q_s, k_s, v_s = qkv_s[:, 0:d], qkv_s[:, d:2 * d], qkv_s[:, 2 * d:3 * d]
            os_ = _moba_sample(q_s, k_s, v_s, cache_k, cache_v, page_table, j, slopes)
            yp = _proj_ln(op, wo, None, yp, ln1_g[i], ln1_b[i])
            ys = _proj_ln(os_, wo, None, ys, ln1_g[i], ln1_b[i])
            k_sa.append(k_s.reshape(bs, ss, N_HEADS, HEAD_DIM))
            v_sa.append(v_s.reshape(bs, ss, N_HEADS, HEAD_DIM))
        else:
            w1 = w_pw1[j].astype(BF16)
            w2 = w_pw2[j].astype(BF16)
            up = _pw1_glu(yp, w1, b_pw1[j])
            zp = _dwconv_ln_prompt(up, w_dw[j], b_dw[j], ln_cv_g[j], ln_cv_b[j], sp)
            us = _pw1_glu(ys, w1, b_pw1[j])
            zs = _dwconv_ln_sample(state_conv[j], us, w_dw[j], b_dw[j], ln_cv_g[j], ln_cv_b[j])
            yp = _proj_ln(zp, w2, b_pw2[j], yp, ln1_g[i], ln1_b[i])
            ys = _proj_ln(zs, w2, b_pw2[j], ys, ln1_g[i], ln1_b[i])
            c_pr.append(up.reshape(bp, sp, d)[:, sp - (CONV_W - 1):, :])
            c_sa.append(jnp.concatenate([state_conv[j][:, 1:, :], us[:, None, :]], axis=1))
        wu = w_up[i].astype(BF16)
        wd = w_down[i].astype(BF16)
        yp, fp = _ffn_prompt(yp, wu, w_fdw[i], b_fdw[i], wd, ln2_g[i], ln2_b[i], bp, sp)
        ys, fs = _ffn_sample(ys, state_ffn[i], wu, w_fdw[i], b_fdw[i], wd, ln2_g[i], ln2_b[i])
        f_pr.append(fp)
        f_sa.append(fs)
    return (yp.reshape(bp, sp, d), ys.reshape(bs, ss, d), kv_pr[0].reshape(kv_shape), kv_pr[1].reshape(kv_shape),
            jnp.stack(k_sa),
            jnp.stack(v_sa), jnp.stack(c_pr), jnp.stack(c_sa), jnp.stack(f_pr), jnp.stack(f_sa))
```

```python
import functools
import math

import numpy as np
import jax
import jax.numpy as jnp
from jax import lax
from jax.experimental import pallas as pl
from jax.experimental.pallas import tpu as pltpu

N_HEADS = 8
HEAD_DIM = 128
D_MODEL = N_HEADS * HEAD_DIM
BLOCK = 256
TOPK = 3
CONV_W = 31
FFN_CONV_W = 3
DEPTH = 4
ALPHA = (2.0 * DEPTH) ** 0.25
LN_EPS = 1e-5
NEG = -1e30
SCALE = 1.0 / math.sqrt(HEAD_DIM)

LANES = 128
SUBLANES = 8
VMEM_LIMIT = 56 * 1024 * 1024

F32 = jnp.float32
BF16 = jnp.bfloat16

TM_ROWS = 512
TM_CONV = 256
CONV_HALO = 32
CONV_R = 64
CONV_C = 256
MOBA_QT = 1024
MOBA_KT = 1024
TM_FFN = 512
FC_FFN = 1408
FFN_SUB = 2
FFN_R = 32
FFN_C = 128
FC_FFN_S = 256
TB_CONV_S = 32

_NT = (((1,), (1,)), ((), ()))


def _alibi_slopes():
    s = np.exp2(-8.0 * (np.arange(N_HEADS, dtype=np.float64) + 1.0) / N_HEADS).astype(np.float32)
    assert np.all(s.astype(jnp.bfloat16).astype(np.float32) == s)
    assert np.all(np.log2(s) == np.round(np.log2(s)))
    return s


def _params(*sem):
    return pltpu.CompilerParams(dimension_semantics=sem, vmem_limit_bytes=VMEM_LIMIT)


def _layer_norm(x, g, b):
    mu = jnp.mean(x, axis=-1, keepdims=True)
    xc = x - mu
    var = jnp.mean(xc * xc, axis=-1, keepdims=True)
    return xc * lax.rsqrt(var + LN_EPS) * g + b


def _silu(x):
    return x * jax.nn.sigmoid(x)


def _qkv_prompt_kernel(n_aliased, slopes_ref, x_ref, w_ref, *refs):
    qa_ref, k_ref, v_ref, kf_ref, vf_ref, kmt_ref = refs[n_aliased:]
    i = pl.program_id(1)
    tm = x_ref.shape[0]
    nb = kmt_ref.shape[0]
    bpt = tm // BLOCK
    xb = x_ref[...].astype(BF16)
    k = jnp.dot(xb, w_ref[:, D_MODEL:2 * D_MODEL], preferred_element_type=F32)
    kf_ref[...] = pltpu.einshape("m(hd)->mhd", k, h=N_HEADS)
    for h in range(N_HEADS):
        k_ref[h] = k[:, h * HEAD_DIM:(h + 1) * HEAD_DIM].astype(BF16)

    @pl.when(i == 0)
    def _():
        kmt_ref[...] = jnp.zeros_like(kmt_ref)

    kmt = kmt_ref[...]
    blk = lax.broadcasted_iota(jnp.int32, kmt.shape, 0)
    for j in range(bpt):
        kmt = jnp.where(blk == bpt * i + j, jnp.mean(k[j * BLOCK:(j + 1) * BLOCK], axis=0, keepdims=True), kmt)
    kmt_ref[...] = kmt

    q = jnp.dot(xb, w_ref[:, 0:D_MODEL], preferred_element_type=F32)
    qs = q * SCALE
    eye = jnp.where(lax.broadcasted_iota(jnp.int32, (BLOCK, BLOCK), 0)
                    == lax.broadcasted_iota(jnp.int32, (BLOCK, BLOCK), 1), 1.0, 0.0).astype(BF16)
    for h in range(N_HEADS):
        sl = slice(h * HEAD_DIM, (h + 1) * HEAD_DIM)
        qa_ref[h] = _query_operand(q[:, sl], qs[:, sl], kmt[:, sl], slopes_ref[h], bpt * i, nb, eye)
    v = jnp.dot(xb, w_ref[:, 2 * D_MODEL:3 * D_MODEL], preferred_element_type=F32)
    vf_ref[...] = pltpu.einshape("m(hd)->mhd", v, h=N_HEADS)
    ones = jnp.ones((tm, HEAD_DIM), BF16)
    for h in range(N_HEADS):
        v_ref[h] = jnp.concatenate([v[:, h * HEAD_DIM:(h + 1) * HEAD_DIM].astype(BF16), ones], axis=1)


def _qkv_prompt(x, w, nb_batch, seq, layer, kv_prev, slopes):
    m, d = x.shape
    tm = TM_ROWS
    assert seq % tm == 0 and tm % BLOCK == 0 and d == D_MODEL and seq // BLOCK + 4 <= HEAD_DIM
    nt = seq // tm
    n_attn = (DEPTH + 1) // 2
    hm = jax.ShapeDtypeStruct((nb_batch, N_HEADS, seq, HEAD_DIM), BF16)
    hm_spec = pl.BlockSpec((None, N_HEADS, tm, HEAD_DIM), lambda b, i: (b, 0, i, 0))
    hm2 = jax.ShapeDtypeStruct((nb_batch, N_HEADS, seq, 2 * HEAD_DIM), BF16)
    hm2_spec = pl.BlockSpec((None, N_HEADS, tm, 2 * HEAD_DIM), lambda b, i: (b, 0, i, 0))
    row_spec = pl.BlockSpec((tm, d), lambda b, i: (b * nt + i, 0))
    kv = jax.ShapeDtypeStruct((n_attn, m, N_HEADS, HEAD_DIM), F32)
    kv_spec = pl.BlockSpec((None, tm, N_HEADS, HEAD_DIM), lambda b, i: (layer, b * nt + i, 0, 0))
    in_specs = [pl.BlockSpec(memory_space=pltpu.SMEM), row_spec, pl.BlockSpec((d, 3 * d), lambda b, i: (0, 0))]
    args = [slopes, x, w]
    aliases = {}
    if kv_prev is not None:
        in_specs += [pl.BlockSpec(memory_space=pl.ANY)] * 2
        args += list(kv_prev)
        aliases = {3: 3, 4: 4}
    return pl.pallas_call(
        functools.partial(_qkv_prompt_kernel, len(aliases)),
        grid=(nb_batch, nt),
        in_specs=in_specs,
        out_specs=[hm2_spec, hm_spec, hm2_spec, kv_spec, kv_spec],
        out_shape=[hm2, hm, hm2, kv, kv],
        scratch_shapes=[pltpu.VMEM((seq // BLOCK, d), F32)],
        input_output_aliases=aliases,
        compiler_params=_params("parallel", "arbitrary"),
        name="qkv_prompt",
    )(*args)


def _mm_kernel(x_ref, w_ref, o_ref):
    o_ref[...] = jnp.dot(x_ref[...].astype(BF16), w_ref[...], preferred_element_type=F32)


def _mm(x, w, tn=1024):
    m, k = x.shape
    n = w.shape[1]
    assert n % tn == 0
    return pl.pallas_call(
        _mm_kernel,
        grid=(n // tn,),
        in_specs=[pl.BlockSpec((m, k), lambda j: (0, 0)), pl.BlockSpec((k, tn), lambda j: (0, j))],
        out_specs=pl.BlockSpec((m, tn), lambda j: (0, j)),
        out_shape=jax.ShapeDtypeStruct((m, n), F32),
        compiler_params=_params("parallel"),
        name="mm_rows",
    )(x, w)


def _proj_ln_kernel(has_bias, a_ref, w_ref, *refs):
    if has_bias:
        bias_ref, y_ref, g_ref, b_ref, o_ref = refs
    else:
        y_ref, g_ref, b_ref, o_ref = refs
    mp = jnp.dot(a_ref[...].astype(BF16), w_ref[...], preferred_element_type=F32)
    if has_bias:
        mp = mp + bias_ref[...]
    o_ref[...] = _layer_norm(ALPHA * y_ref[...] + mp, g_ref[...], b_ref[...])


def _proj_ln(a, w, bias, y, g, b):
    m, d = y.shape
    tm = min(TM_ROWS, m)
    assert m % tm == 0
    row = lambda i: (i, 0)
    fixed = lambda i: (0, 0)
    vec = pl.BlockSpec((1, d), fixed)
    in_specs = [pl.BlockSpec((tm, a.shape[1]), row), pl.BlockSpec(w.shape, fixed)]
    args = [a, w]
    if bias is not None:
        in_specs.append(vec)
        args.append(bias.reshape(1, d))
    in_specs += [pl.BlockSpec((tm, d), row), vec, vec]
    args += [y, g.reshape(1, d), b.reshape(1, d)]
    return pl.pallas_call(
        functools.partial(_proj_ln_kernel, bias is not None),
        grid=(m // tm,),
        in_specs=in_specs,
        out_specs=pl.BlockSpec((tm, d), row),
        out_shape=jax.ShapeDtypeStruct((m, d), F32),
        compiler_params=_params("parallel"),
        name="proj_ln",
    )(*args)


def _pw1_glu_kernel(x_ref, w_ref, b_ref, u_ref):
    d = u_ref.shape[1]
    xb = x_ref[...].astype(BF16)
    a = jnp.dot(xb, w_ref[:, 0:d], preferred_element_type=F32) + b_ref[:, 0:d]
    gt = jnp.dot(xb, w_ref[:, d:2 * d], preferred_element_type=F32) + b_ref[:, d:2 * d]
    u_ref[...] = a * jax.nn.sigmoid(gt)


def _pw1_glu(x, w, bias):
    m, d = x.shape
    tm = min(TM_ROWS, m)
    assert m % tm == 0
    return pl.pallas_call(
        _pw1_glu_kernel,
        grid=(m // tm,),
        in_specs=[pl.BlockSpec((tm, d), lambda i: (i, 0)), pl.BlockSpec((d, 2 * d), lambda i: (0, 0)),
                  pl.BlockSpec((1, 2 * d), lambda i: (0, 0))],
        out_specs=pl.BlockSpec((tm, d), lambda i: (i, 0)),
        out_shape=jax.ShapeDtypeStruct((m, d), F32),
        compiler_params=_params("parallel"),
        name="pw1_glu",
    )(x, w, bias.reshape(1, 2 * d))


def _dwconv_ln_prompt_kernel(tiles_per_seq, u_ref, halo_ref, w_ref, b_ref, g_ref, be_ref, o_ref, ext_ref, y_ref):
    i = pl.program_id(0)
    tm, d = u_ref.shape
    first = (i % tiles_per_seq) == 0
    ext_ref[0:CONV_HALO, :] = jnp.where(first, 0.0, halo_ref[...])
    ext_ref[CONV_HALO:CONV_HALO + tm, :] = u_ref[...]
    off = CONV_HALO - (CONV_W - 1)

    def row_body(r, carry):
        r0 = pl.multiple_of(r * CONV_R, CONV_R)
        for c in range(d // CONV_C):
            cs = slice(c * CONV_C, (c + 1) * CONV_C)
            win = ext_ref[pl.ds(r0, CONV_R + CONV_HALO), cs]
            acc = jnp.zeros((CONV_R, CONV_C), F32)
            for ph in range(SUBLANES):
                taps = [k for k in range(CONV_W) if (off + k) % SUBLANES == ph]
                if not taps:
                    continue
                wp = win if ph == 0 else pltpu.roll(win, CONV_R + CONV_HALO - ph, 0)
                for k in taps:
                    a = off + k - ph
                    acc = acc + wp[a:a + CONV_R, :] * w_ref[k:k + 1, cs]
            y_ref[pl.ds(r0, CONV_R), cs] = acc + b_ref[:, cs]
        return carry

    lax.fori_loop(0, tm // CONV_R, row_body, 0)
    z = _layer_norm(y_ref[...], g_ref[...], be_ref[...])
    o_ref[...] = _silu(z).astype(BF16)


def _dwconv_ln_prompt(u, w, bias, g, b, seq):
    m, d = u.shape
    tm = TM_CONV
    assert seq % tm == 0 and tm % CONV_R == 0 and tm % CONV_HALO == 0 and d % CONV_C == 0
    assert CONV_HALO >= CONV_W - 1 and CONV_HALO % SUBLANES == 0
    hb = tm // CONV_HALO
    vec = pl.BlockSpec((1, d), lambda i: (0, 0))
    return pl.pallas_call(
        functools.partial(_dwconv_ln_prompt_kernel, seq // tm),
        grid=(m // tm,),
        in_specs=[pl.BlockSpec((tm, d), lambda i: (i, 0)),
                  pl.BlockSpec((CONV_HALO, d), lambda i: (jnp.maximum(i * hb - 1, 0), 0)),
                  pl.BlockSpec((CONV_W, d), lambda i: (0, 0)), vec, vec, vec],
        out_specs=pl.BlockSpec((tm, d), lambda i: (i, 0)),
        out_shape=jax.ShapeDtypeStruct((m, d), BF16),
        scratch_shapes=[pltpu.VMEM((tm + CONV_HALO, d), F32), pltpu.VMEM((tm, d), F32)],
        compiler_params=_params("parallel"),
        name="dwconv_ln_prompt",
    )(u, u, w, bias.reshape(1, d), g.reshape(1, d), b.reshape(1, d))


def _dwconv_ln_sample_kernel(st_ref, u_ref, w_ref, b_ref, g_ref, be_ref, o_ref):
    st = st_ref[...]
    w = w_ref[...]
    y = jnp.sum(st * w[None, 0:CONV_W - 1, :], axis=1)
    y = y + u_ref[...] * w[CONV_W - 1:CONV_W, :] + b_ref[...]
    z = _layer_norm(y, g_ref[...], be_ref[...])
    o_ref[...] = _silu(z).astype(BF16)


def _dwconv_ln_sample(state, u, w, bias, g, b):
    nb, hist, d = state.shape
    tb = min(TB_CONV_S, nb)
    assert nb % tb == 0 and hist == CONV_W - 1
    vec = pl.BlockSpec((1, d), lambda i: (0, 0))
    return pl.pallas_call(
        _dwconv_ln_sample_kernel,
        grid=(nb // tb,),
        in_specs=[pl.BlockSpec((tb, hist, d), lambda i: (i, 0, 0)), pl.BlockSpec((tb, d), lambda i: (i, 0)),
                  pl.BlockSpec((CONV_W, d), lambda i: (0, 0)), vec, vec, vec],
        out_specs=pl.BlockSpec((tb, d), lambda i: (i, 0)),
        out_shape=jax.ShapeDtypeStruct((nb, d), BF16),
        compiler_params=_params("parallel"),
        name="dwconv_ln_sample",
    )(state, u, w, bias.reshape(1, d), g.reshape(1, d), b.reshape(1, d))


def _aug_lanes(nb):
    return dict(tq=nb, r=nb + 1, kb=nb + 2, qb=nb + 3)


def _query_operand(q, qs, km, slope, first_block, nb, eye):
    rows = q.shape[0]
    ln = _aug_lanes(nb)
    gate = lax.dot_general(km.astype(BF16), q.astype(BF16), _NT, preferred_element_type=F32)
    blk = lax.broadcasted_iota(jnp.int32, (nb, rows), 0)
    own_t = first_block + lax.broadcasted_iota(jnp.int32, (nb, rows), 1) // BLOCK
    past = blk < own_t
    gw = jnp.where(past, gate, -jnp.inf)
    valid_bias = jnp.where(past, 0.0, NEG)
    sel_t = jnp.full((nb, rows), NEG, F32)
    blkf = blk.astype(F32)
    for _ in range(TOPK):
        mx = jnp.max(gw, axis=0, keepdims=True)
        idx = jnp.min(jnp.where(gw == mx, blkf, float(nb)), axis=0, keepdims=True)
        pick = blkf == idx
        sel_t = jnp.where(pick, valid_bias, sel_t)
        gw = jnp.where(pick, -jnp.inf, gw)
    sel_t = jnp.where(blk == own_t, 0.0, sel_t)
    sel_p = jnp.concatenate([sel_t.astype(BF16), jnp.zeros((HEAD_DIM - nb, rows), BF16)], axis=0)
    selbias = jnp.concatenate(
        [lax.dot_general(eye, sel_p[:, a * BLOCK:(a + 1) * BLOCK], _NT, preferred_element_type=F32)
         for a in range(rows // BLOCK)], axis=0)

    lane = lax.broadcasted_iota(jnp.int32, (rows, HEAD_DIM), 1)
    rowi = lax.broadcasted_iota(jnp.int32, (rows, HEAD_DIM), 0)
    own = first_block + rowi // BLOCK
    aug = jnp.where(lane < nb, selbias, 0.0)
    aug = jnp.where(lane == ln["tq"], -slope * (rowi % BLOCK).astype(F32), aug)
    aug = jnp.where((lane == ln["r"]) | (lane == ln["kb"]), 1.0, aug)
    aug = jnp.where(lane == ln["qb"], -slope * (BLOCK * own).astype(F32), aug)
    return jnp.concatenate([qs.astype(BF16), aug.astype(BF16)], axis=1)


def _moba_prompt_kernel(slopes_ref, qa_ref, k_ref, v_ref, o_ref, tab_ref, acc_ref):
    h = pl.program_id(1)
    i = pl.program_id(2)
    qt = qa_ref.shape[0]
    nsub = qt // BLOCK
    nb = k_ref.shape[0] // BLOCK
    ln = _aug_lanes(nb)
    slope = slopes_ref[h]

    @pl.when(i == 0)
    def _build_key_table():
        lane_b = lax.broadcasted_iota(jnp.int32, (BLOCK, HEAD_DIM), 1)
        rowf_b = lax.broadcasted_iota(jnp.int32, (BLOCK, HEAD_DIM), 0).astype(F32)
        base = jnp.where((lane_b == ln["tq"]) | (lane_b == ln["qb"]), 1.0, 0.0)
        base = jnp.where(lane_b == ln["r"], slope * rowf_b, base)

        def tab_body(b, carry):
            t = jnp.where(lane_b == b, 1.0, base)
            t = jnp.where(lane_b == ln["kb"], slope * jnp.asarray(BLOCK * b, F32), t)
            tab_ref[pl.ds(pl.multiple_of(b * BLOCK, BLOCK), BLOCK), :] = t.astype(BF16)
            return carry

        lax.fori_loop(0, nb, tab_body, 0)

    qas = [qa_ref[a * BLOCK:(a + 1) * BLOCK, :] for a in range(nsub)]

    def key_tile(j):
        rows = pl.ds(pl.multiple_of(j * MOBA_KT, MOBA_KT), MOBA_KT)
        return jnp.concatenate([k_ref[rows, :], tab_ref[rows, :]], axis=1), v_ref[rows, :]

    ka, vb = key_tile(i)
    ms = []
    for a in range(nsub):
        nk = (a + 1) * BLOCK
        s = lax.dot_general(qas[a], ka[0:nk, :], _NT, preferred_element_type=F32)
        diff = (lax.broadcasted_iota(jnp.int32, (BLOCK, nk), 1)
                - lax.broadcasted_iota(jnp.int32, (BLOCK, nk), 0))
        s = jnp.where(diff <= a * BLOCK, s, NEG)
        m0 = jnp.max(s, axis=1, keepdims=True)
        ms.append(m0)
        acc_ref[a] = jnp.dot(jnp.exp(s - m0).astype(BF16), vb[0:nk, :], preferred_element_type=F32)

    def tile_body(j, ms):
        ka, vb = key_tile(j)
        out = []
        for a in range(nsub):
            s = lax.dot_general(qas[a], ka, _NT, preferred_element_type=F32)
            m_new = jnp.maximum(ms[a], jnp.max(s, axis=1, keepdims=True))
            out.append(m_new)
            acc_ref[a] = (jnp.exp(ms[a] - m_new) * acc_ref[a]
                          + jnp.dot(jnp.exp(s - m_new).astype(BF16), vb, preferred_element_type=F32))
        return tuple(out)

    lax.fori_loop(0, i, tile_body, tuple(ms))
    for a in range(nsub):
        acc = acc_ref[a]
        o_ref[a * BLOCK:(a + 1) * BLOCK, :] = (acc[:, 0:HEAD_DIM] / acc[:, HEAD_DIM:2 * HEAD_DIM]).astype(BF16)


def _moba_prompt(qa, k, v, slopes):
    nbt, nh, seq, hd = k.shape
    qt = MOBA_QT
    assert seq % qt == 0 and MOBA_KT == qt and qt % BLOCK == 0
    assert hd == HEAD_DIM == LANES and nh == N_HEADS and v.shape[-1] == 2 * hd and qa.shape[-1] == 2 * hd
    nq = seq // qt
    assert seq // BLOCK + 4 <= HEAD_DIM
    qspec = pl.BlockSpec((None, None, qt, 2 * hd), lambda b, h, i: (b, h, i, 0))
    kspec = pl.BlockSpec((None, None, seq, hd), lambda b, h, i: (b, h, 0, 0))
    vspec = pl.BlockSpec((None, None, seq, 2 * hd), lambda b, h, i: (b, h, 0, 0))
    return pl.pallas_call(
        _moba_prompt_kernel,
        grid=(nbt, nh, nq),
        in_specs=[pl.BlockSpec(memory_space=pltpu.SMEM), qspec, kspec, vspec],
        out_specs=pl.BlockSpec((qt, hd), lambda b, h, i: (b * nq + i, h)),
        out_shape=jax.ShapeDtypeStruct((nbt * seq, nh * hd), BF16),
        scratch_shapes=[pltpu.VMEM((seq, hd), BF16), pltpu.VMEM((qt // BLOCK, BLOCK, 2 * hd), F32)],
        compiler_params=_params("parallel", "parallel", "arbitrary"),
        name="moba_prompt",
    )(slopes, qa, k, v)


def _moba_sample_kernel(n_pages, pt_ref, slopes_ref, q_ref, kn_ref, vn_ref, *refs):
    del pt_ref
    k_refs = refs[:n_pages]
    v_refs = refs[n_pages:2 * n_pages]
    o_ref = refs[2 * n_pages]
    base_ref = refs[2 * n_pages + 1]
    page = k_refs[0].shape[0]
    past_len = n_pages * page
    nbp = past_len // BLOCK
    ppb = BLOCK // page
    lpp = page * N_HEADS
    lpb = BLOCK * N_HEADS
    lane_b = lax.broadcasted_iota(jnp.int32, (N_HEADS, lpb), 1)
    row_b = lax.broadcasted_iota(jnp.int32, (N_HEADS, lpb), 0)
    own_head = (lane_b % N_HEADS) == row_b

    @pl.when(pl.program_id(0) == 0)
    def _build_bias():
        slopes = slopes_ref[...]
        for b in range(nbp):
            dist = (past_len - b * BLOCK - lane_b // N_HEADS).astype(F32)
            base_ref[:, b * lpb:(b + 1) * lpb] = jnp.where(own_head, -slopes * dist, NEG)

    q8 = q_ref[...]
    qb = q8.astype(BF16)

    raws = []
    for b in range(nbp):
        parts = []
        for pg in range(b * ppb, (b + 1) * ppb):
            kf = k_refs[pg][...].reshape(lpp, HEAD_DIM).astype(BF16)
            parts.append(lax.dot_general(qb, kf, _NT, preferred_element_type=F32))
        raws.append(jnp.concatenate(parts, axis=1))

    lane = lax.broadcasted_iota(jnp.int32, (N_HEADS, LANES), 1)
    gw = jnp.full((N_HEADS, LANES), -jnp.inf, F32)
    for b in range(nbp):
        gb = jnp.sum(jnp.where(own_head, raws[b], 0.0), axis=1, keepdims=True) * (1.0 / BLOCK)
        gw = jnp.where(lane == b, gb, gw)
    sel = jnp.zeros((N_HEADS, LANES), F32)
    lanef = lane.astype(F32)
    for _ in range(min(TOPK, nbp)):
        mx = jnp.max(gw, axis=1, keepdims=True)
        idx = jnp.min(jnp.where(gw == mx, lanef, float(LANES)), axis=1, keepdims=True)
        pick = lanef == idx
        sel = jnp.where(pick, 1.0, sel)
        gw = jnp.where(pick, -jnp.inf, gw)

    scores = [jnp.where(sel[:, b:b + 1] > 0.5, raws[b] * SCALE + base_ref[:, b * lpb:(b + 1) * lpb], NEG)
              for b in range(nbp)]
    s_new = jnp.sum(q8 * kn_ref[...], axis=1, keepdims=True) * SCALE

    m = s_new
    for b in range(nbp):
        m = jnp.maximum(m, jnp.max(scores[b], axis=1, keepdims=True))
    p_new = jnp.exp(s_new - m)
    l = p_new
    acc = p_new * vn_ref[...]
    for b in range(nbp):
        p = jnp.exp(scores[b] - m)
        l = l + jnp.sum(p, axis=1, keepdims=True)
        pb = p.astype(BF16)
        for t in range(ppb):
            vf = v_refs[b * ppb + t][...].reshape(lpp, HEAD_DIM).astype(BF16)
            acc = acc + jnp.dot(pb[:, t * lpp:(t + 1) * lpp], vf, preferred_element_type=F32)
    o_ref[...] = acc / l


def _moba_sample(q, k_new, v_new, cache_k, cache_v, page_table, layer, slopes):
    nb, d = q.shape
    n_pages = page_table.shape[1]
    page = cache_k.shape[2]
    assert BLOCK % page == 0 and (n_pages * page) % BLOCK == 0 and d == D_MODEL
    assert cache_k.shape[3:] == (N_HEADS, HEAD_DIM) and N_HEADS == SUBLANES
    row = pl.BlockSpec((None, N_HEADS, HEAD_DIM), lambda b, pt: (b, 0, 0))

    def page_spec(p):
        return pl.BlockSpec((None, None, page, N_HEADS, HEAD_DIM), lambda b, pt: (layer, pt[b, p], 0, 0, 0))

    heads = lambda t: t.reshape(nb, N_HEADS, HEAD_DIM)
    out = pl.pallas_call(
        functools.partial(_moba_sample_kernel, n_pages),
        grid_spec=pltpu.PrefetchScalarGridSpec(
            num_scalar_prefetch=1,
            grid=(nb,),
            in_specs=[pl.BlockSpec((N_HEADS, 1), lambda b, pt: (0, 0)), row, row, row]
                     + [page_spec(p) for p in range(n_pages)] * 2,
            out_specs=row,
            scratch_shapes=[pltpu.VMEM((N_HEADS, n_pages * page * N_HEADS), F32)],
        ),
        out_shape=jax.ShapeDtypeStruct((nb, N_HEADS, HEAD_DIM), F32),
        compiler_params=_params("arbitrary"),
        name="moba_sample",
    )(page_table, slopes.reshape(N_HEADS, 1), heads(q), heads(k_new), heads(v_new),
      *([cache_k] * n_pages), *([cache_v] * n_pages))
    return out.reshape(nb, d)


def _ffn_prompt_kernel(tiles_per_seq, x_ref, wg_ref, wu_ref, fwg_ref, fwu_ref, fbg_ref, fbu_ref, wd_ref,
                       g_ref, b_ref, y_ref, st_ref, xb_ref, acc_ref, hg_ref, hu_ref, act_ref, cg_ref, cu_ref):
    i = pl.program_id(0)
    c = pl.program_id(1)
    tm = x_ref.shape[0]
    fc = wg_ref.shape[1]
    first = (i % tiles_per_seq) == 0
    hist = FFN_CONV_W - 1

    @pl.when(c == 0)
    def _():
        xb_ref[...] = x_ref[...].astype(BF16)

    for h_ref, carry_ref in ((hg_ref, cg_ref), (hu_ref, cu_ref)):
        @pl.when(first)
        def _():
            h_ref[0:SUBLANES, :] = jnp.zeros((SUBLANES, fc), F32)

        @pl.when(jnp.logical_not(first))
        def _():
            h_ref[0:SUBLANES, :] = carry_ref[c]

    def conv(h_ref, fw_ref, fb_ref, r0, cs):
        win = h_ref[r0:r0 + FFN_R + SUBLANES, cs]
        cur = slice(SUBLANES, SUBLANES + FFN_R)
        return (fw_ref[0:1, cs] * pltpu.roll(win, 2, 0)[cur, :]
                + fw_ref[1:2, cs] * pltpu.roll(win, 1, 0)[cur, :]
                + fw_ref[2:3, cs] * win[cur, :] + fb_ref[:, cs])

    sr = tm // FFN_SUB
    downs = []
    for sub in range(FFN_SUB):
        rs = slice(sub * sr, (sub + 1) * sr)
        hs = slice(SUBLANES + sub * sr, SUBLANES + (sub + 1) * sr)
        xs = xb_ref[rs, :]
        hg_ref[hs, :] = jnp.dot(xs, wg_ref[...], preferred_element_type=F32)
        hu_ref[hs, :] = jnp.dot(xs, wu_ref[...], preferred_element_type=F32)
        for r0 in range(sub * sr, (sub + 1) * sr, FFN_R):
            for cc in range(fc // FFN_C):
                cs = slice(cc * FFN_C, (cc + 1) * FFN_C)
                gt = conv(hg_ref, fwg_ref, fbg_ref, r0, cs)
                u = conv(hu_ref, fwu_ref, fbu_ref, r0, cs)
                act_ref[r0:r0 + FFN_R, cs] = (_silu(gt) * u).astype(BF16)
        downs.append(jnp.dot(act_ref[rs, :], wd_ref[...], preferred_element_type=F32))
    down = jnp.concatenate(downs, axis=0)

    cg_ref[c] = hg_ref[tm:tm + SUBLANES, :]
    cu_ref[c] = hu_ref[tm:tm + SUBLANES, :]
    for r in range(hist):
        row = SUBLANES + tm - hist + r
        st_ref[r] = jnp.concatenate([hg_ref[row:row + 1, :], hu_ref[row:row + 1, :]], axis=0)
    last = pl.num_programs(1) - 1

    @pl.when(jnp.logical_and(c == 0, c != last))
    def _():
        acc_ref[...] = down

    @pl.when(jnp.logical_and(c > 0, c != last))
    def _():
        acc_ref[...] += down

    @pl.when(jnp.logical_and(c == last, c > 0))
    def _():
        y_ref[...] = _layer_norm(ALPHA * x_ref[...] + (acc_ref[...] + down), g_ref[...], b_ref[...])

    @pl.when(jnp.logical_and(c == last, c == 0))
    def _():
        y_ref[...] = _layer_norm(ALPHA * x_ref[...] + down, g_ref[...], b_ref[...])


def _ffn_prompt(x, w_up, w_fdw, b_fdw, w_down, g, b, nb_batch, seq):
    m, d = x.shape
    f = w_down.shape[0]
    tm = TM_FFN
    fc = FC_FFN if f % FC_FFN == 0 else FFN_C
    assert seq % tm == 0 and f % fc == 0 and FFN_CONV_W == 3 and tm % FFN_R == 0 and fc % FFN_C == 0
    nc = f // fc
    tps = seq // tm
    vec = pl.BlockSpec((1, d), lambda i, c: (0, 0))
    y, st = pl.pallas_call(
        functools.partial(_ffn_prompt_kernel, tps),
        grid=(m // tm, nc),
        in_specs=[pl.BlockSpec((tm, d), lambda i, c: (i, 0)),
                  pl.BlockSpec((d, fc), lambda i, c: (0, c)), pl.BlockSpec((d, fc), lambda i, c: (0, nc + c)),
                  pl.BlockSpec((FFN_CONV_W, fc), lambda i, c: (0, c)),
                  pl.BlockSpec((FFN_CONV_W, fc), lambda i, c: (0, nc + c)),
                  pl.BlockSpec((1, fc), lambda i, c: (0, c)), pl.BlockSpec((1, fc), lambda i, c: (0, nc + c)),
                  pl.BlockSpec((fc, d), lambda i, c: (c, 0)), vec, vec],
        out_specs=[pl.BlockSpec((tm, d), lambda i, c: (i, 0)),
                   pl.BlockSpec((None, FFN_CONV_W - 1, 2, fc), lambda i, c: (i, 0, 0, c))],
        out_shape=[jax.ShapeDtypeStruct((m, d), F32),
                   jax.ShapeDtypeStruct((m // tm, FFN_CONV_W - 1, 2, f), F32)],
        scratch_shapes=[pltpu.VMEM((tm, d), BF16), pltpu.VMEM((tm, d), F32),
                        pltpu.VMEM((tm + SUBLANES, fc), F32), pltpu.VMEM((tm + SUBLANES, fc), F32),
                        pltpu.VMEM((tm, fc), BF16),
                        pltpu.VMEM((nc, SUBLANES, fc), F32), pltpu.VMEM((nc, SUBLANES, fc), F32)],
        compiler_params=_params("arbitrary", "arbitrary"),
        name="ffn_prompt",
    )(x, w_up, w_up, w_fdw, w_fdw, b_fdw.reshape(1, 2 * f), b_fdw.reshape(1, 2 * f), w_down,
      g.reshape(1, d), b.reshape(1, d))
    return y, st[tps - 1::tps].reshape(nb_batch, FFN_CONV_W - 1, 2 * f)


def _ffn_sample_kernel(x_ref, s0g_ref, s0u_ref, s1g_ref, s1u_ref, wg_ref, wu_ref, fwg_ref, fwu_ref,
                       fbg_ref, fbu_ref, wd_ref, g_ref, b_ref, y_ref, hg_ref, hu_ref, acc_ref):
    c = pl.program_id(0)

    @pl.when(c == 0)
    def _():
        acc_ref[...] = jnp.zeros_like(acc_ref)

    xb = x_ref[...].astype(BF16)
    hg = jnp.dot(xb, wg_ref[...], preferred_element_type=F32)
    hu = jnp.dot(xb, wu_ref[...], preferred_element_type=F32)
    hg_ref[...] = hg
    hu_ref[...] = hu
    gt = fwg_ref[0:1, :] * s0g_ref[...] + fwg_ref[1:2, :] * s1g_ref[...] + fwg_ref[2:3, :] * hg + fbg_ref[...]
    u = fwu_ref[0:1, :] * s0u_ref[...] + fwu_ref[1:2, :] * s1u_ref[...] + fwu_ref[2:3, :] * hu + fbu_ref[...]
    act = (_silu(gt) * u).astype(BF16)
    acc_ref[...] += jnp.dot(act, wd_ref[...], preferred_element_type=F32)

    @pl.when(c == pl.num_programs(0) - 1)
    def _():
        y_ref[...] = _layer_norm(ALPHA * x_ref[...] + acc_ref[...], g_ref[...], b_ref[...])


def _ffn_sample(x, state, w_up, w_fdw, b_fdw, w_down, g, b):
    nb, d = x.shape
    f = w_down.shape[0]
    fc = FC_FFN_S
    assert f % fc == 0 and FFN_CONV_W == 3 and state.shape == (nb, FFN_CONV_W - 1, 2 * f)
    nc = f // fc
    st = state.reshape(nb, 4 * f)
    vec = pl.BlockSpec((1, d), lambda c: (0, 0))
    col = lambda k: pl.BlockSpec((nb, fc), lambda c: (0, k * nc + c))
    y, hg, hu = pl.pallas_call(
        _ffn_sample_kernel,
        grid=(nc,),
        in_specs=[pl.BlockSpec((nb, d), lambda c: (0, 0)), col(0), col(1), col(2), col(3),
                  pl.BlockSpec((d, fc), lambda c: (0, c)), pl.BlockSpec((d, fc), lambda c: (0, nc + c)),
                  pl.BlockSpec((FFN_CONV_W, fc), lambda c: (0, c)),
                  pl.BlockSpec((FFN_CONV_W, fc), lambda c: (0, nc + c)),
                  pl.BlockSpec((1, fc), lambda c: (0, c)), pl.BlockSpec((1, fc), lambda c: (0, nc + c)),
                  pl.BlockSpec((fc, d), lambda c: (c, 0)), vec, vec],
        out_specs=[pl.BlockSpec((nb, d), lambda c: (0, 0)), pl.BlockSpec((nb, fc), lambda c: (0, c)),
                   pl.BlockSpec((nb, fc), lambda c: (0, c))],
        out_shape=[jax.ShapeDtypeStruct((nb, d), F32), jax.ShapeDtypeStruct((nb, f), F32),
                   jax.ShapeDtypeStruct((nb, f), F32)],
        scratch_shapes=[pltpu.VMEM((nb, d), F32)],
        compiler_params=_params("arbitrary"),
        name="ffn_sample",
    )(x, st, st, st, st, w_up, w_up, w_fdw, w_fdw, b_fdw.reshape(1, 2 * f), b_fdw.reshape(1, 2 * f), w_down,
      g.reshape(1, d), b.reshape(1, d))
    new_state = jnp.stack([state[:, 1, :], jnp.concatenate([hg, hu], axis=1)], axis=1)
    return y, new_state


def kernel(x_prompt, x_sample, cache_k, cache_v, state_conv, state_ffn, page_table, w_qkv, w_o, w_pw1, b_pw1,
           w_dw, b_dw, ln_cv_g, ln_cv_b, w_pw2, b_pw2, w_up, w_fdw, b_fdw, w_down, ln1_g, ln1_b, ln2_g, ln2_b):
    bp, sp, d = x_prompt.shape
    bs, ss, _ = x_sample.shape
    assert d == D_MODEL and ss == 1
    slopes = jnp.asarray(_alibi_slopes())
    yp = x_prompt.reshape(bp * sp, d)
    ys = x_sample.reshape(bs, d)
    k_sa, v_sa, c_pr, c_sa, f_pr, f_sa = [], [], [], [], [], []
    kv_pr = None
    kv_shape = ((DEPTH + 1) // 2, bp, sp, N_HEADS, HEAD_DIM)
    for i in range(DEPTH):
        j = i // 2
        if i % 2 == 0:
            wq = w_qkv[j].astype(BF16)
            wo = w_o[j].astype(BF16)
            qa, k, v, kf, vf = _qkv_prompt(yp, wq, bp, sp, j, kv_pr, slopes)
            kv_pr = (kf, vf)
            op = _moba_prompt(qa, k, v, slopes)
            qkv_s = _mm(ys, wq)
            q_s, k_s, v_s = qkv_s[:, 0:d], qkv_s[:, d:2 * d], qkv_s[:, 2 * d:3 * d]
            os_ = _moba_sample(q_s, k_s, v_s, cache_k, cache_v, page_table, j, slopes)
            yp = _proj_ln(op, wo, None, yp, ln1_g[i], ln1_b[i])
            ys = _proj_ln(os_, wo, None, ys, ln1_g[i], ln1_b[i])
            k_sa.append(k_s.reshape(bs, ss, N_HEADS, HEAD_DIM))
            v_sa.append(v_s.reshape(bs, ss, N_HEADS, HEAD_DIM))
        else:
            w1 = w_pw1[j].astype(BF16)
            w2 = w_pw2[j].astype(BF16)
            up = _pw1_glu(yp, w1, b_pw1[j])
            zp = _dwconv_ln_prompt(up, w_dw[j], b_dw[j], ln_cv_g[j], ln_cv_b[j], sp)
            us = _pw1_glu(ys, w1, b_pw1[j])
            zs = _dwconv_ln_sample(state_conv[j], us, w_dw[j], b_dw[j], ln_cv_g[j], ln_cv_b[j])
            yp = _proj_ln(zp, w2, b_pw2[j], yp, ln1_g[i], ln1_b[i])
            ys = _proj_ln(zs, w2, b_pw2[j], ys, ln1_g[i], ln1_b[i])
            c_pr.append(up.reshape(bp, sp, d)[:, sp - (CONV_W - 1):, :])
            c_sa.append(jnp.concatenate([state_conv[j][:, 1:, :], us[:, None, :]], axis=1))
        wu = w_up[i].astype(BF16)
        wd = w_down[i].astype(BF16)
        yp, fp = _ffn_prompt(yp, wu, w_fdw[i], b_fdw[i], wd, ln2_g[i], ln2_b[i], bp, sp)
        ys, fs = _ffn_sample(ys, state_ffn[i], wu, w_fdw[i], b_fdw[i], wd, ln2_g[i], ln2_b[i])
        f_pr.append(fp)
        f_sa.append(fs)
    return (yp.reshape(bp, sp, d), ys.reshape(bs, ss, d), kv_pr[0].reshape(kv_shape), kv_pr[1].reshape(kv_shape),
            jnp.stack(k_sa),
            jnp.stack(v_sa), jnp.stack(c_pr), jnp.stack(c_sa), jnp.stack(f_pr), jnp.stack(f_sa))
```

```python
import functools
import math

import numpy as np
import jax
import jax.numpy as jnp
from jax import lax
from jax.experimental import pallas as pl
from jax.experimental.pallas import tpu as pltpu

N_HEADS = 8
HEAD_DIM = 128
D_MODEL = N_HEADS * HEAD_DIM
BLOCK = 256
TOPK = 3
CONV_W = 31
FFN_CONV_W = 3
DEPTH = 4
ALPHA = (2.0 * DEPTH) ** 0.25
LN_EPS = 1e-5
NEG = -1e30
SCALE = 1.0 / math.sqrt(HEAD_DIM)

LANES = 128
SUBLANES = 8
VMEM_LIMIT = 56 * 1024 * 1024

F32 = jnp.float32
BF16 = jnp.bfloat16

TM_ROWS = 512
TM_CONV = 256
CONV_HALO = 32
CONV_R = 64
CONV_C = 128
MOBA_QT = 1024
MOBA_KT = 1024
TM_FFN = 512
FC_FFN = 1408
FFN_SUB = 2
FFN_R = 32
FFN_C = 128
FC_FFN_S = 256
TB_CONV_S = 32

_NT = (((1,), (1,)), ((), ()))


def _alibi_slopes():
    s = np.exp2(-8.0 * (np.arange(N_HEADS, dtype=np.float64) + 1.0) / N_HEADS).astype(np.float32)
    assert np.all(s.astype(jnp.bfloat16).astype(np.float32) == s)
    assert np.all(np.log2(s) == np.round(np.log2(s)))
    return s


def _params(*sem):
    return pltpu.CompilerParams(dimension_semantics=sem, vmem_limit_bytes=VMEM_LIMIT)


def _layer_norm(x, g, b):
    mu = jnp.mean(x, axis=-1, keepdims=True)
    xc = x - mu
    var = jnp.mean(xc * xc, axis=-1, keepdims=True)
    return xc * lax.rsqrt(var + LN_EPS) * g + b


def _silu(x):
    return x * jax.nn.sigmoid(x)


def _qkv_prompt_kernel(n_aliased, slopes_ref, x_ref, w_ref, *refs):
    qa_ref, k_ref, v_ref, kf_ref, vf_ref, kmt_ref = refs[n_aliased:]
    i = pl.program_id(1)
    tm = x_ref.shape[0]
    nb = kmt_ref.shape[0]
    bpt = tm // BLOCK
    xb = x_ref[...].astype(BF16)
    k = jnp.dot(xb, w_ref[:, D_MODEL:2 * D_MODEL], preferred_element_type=F32)
    kf_ref[...] = pltpu.einshape("m(hd)->mhd", k, h=N_HEADS)
    for h in range(N_HEADS):
        k_ref[h] = k[:, h * HEAD_DIM:(h + 1) * HEAD_DIM].astype(BF16)

    @pl.when(i == 0)
    def _():
        kmt_ref[...] = jnp.zeros_like(kmt_ref)

    kmt = kmt_ref[...]
    blk = lax.broadcasted_iota(jnp.int32, kmt.shape, 0)
    for j in range(bpt):
        kmt = jnp.where(blk == bpt * i + j, jnp.mean(k[j * BLOCK:(j + 1) * BLOCK], axis=0, keepdims=True), kmt)
    kmt_ref[...] = kmt

    q = jnp.dot(xb, w_ref[:, 0:D_MODEL], preferred_element_type=F32)
    qs = q * SCALE
    eye = jnp.where(lax.broadcasted_iota(jnp.int32, (BLOCK, BLOCK), 0)
                    == lax.broadcasted_iota(jnp.int32, (BLOCK, BLOCK), 1), 1.0, 0.0).astype(BF16)
    for h in range(N_HEADS):
        sl = slice(h * HEAD_DIM, (h + 1) * HEAD_DIM)
        qa_ref[h] = _query_operand(q[:, sl], qs[:, sl], kmt[:, sl], slopes_ref[h], bpt * i, nb, eye)
    v = jnp.dot(xb, w_ref[:, 2 * D_MODEL:3 * D_MODEL], preferred_element_type=F32)
    vf_ref[...] = pltpu.einshape("m(hd)->mhd", v, h=N_HEADS)
    ones = jnp.ones((tm, HEAD_DIM), BF16)
    for h in range(N_HEADS):
        v_ref[h] = jnp.concatenate([v[:, h * HEAD_DIM:(h + 1) * HEAD_DIM].astype(BF16), ones], axis=1)


def _qkv_prompt(x, w, nb_batch, seq, layer, kv_prev, slopes):
    m, d = x.shape
    tm = TM_ROWS
    assert seq % tm == 0 and tm % BLOCK == 0 and d == D_MODEL and seq // BLOCK + 4 <= HEAD_DIM
    nt = seq // tm
    n_attn = (DEPTH + 1) // 2
    hm = jax.ShapeDtypeStruct((nb_batch, N_HEADS, seq, HEAD_DIM), BF16)
    hm_spec = pl.BlockSpec((None, N_HEADS, tm, HEAD_DIM), lambda b, i: (b, 0, i, 0))
    hm2 = jax.ShapeDtypeStruct((nb_batch, N_HEADS, seq, 2 * HEAD_DIM), BF16)
    hm2_spec = pl.BlockSpec((None, N_HEADS, tm, 2 * HEAD_DIM), lambda b, i: (b, 0, i, 0))
    row_spec = pl.BlockSpec((tm, d), lambda b, i: (b * nt + i, 0))
    kv = jax.ShapeDtypeStruct((n_attn, m, N_HEADS, HEAD_DIM), F32)
    kv_spec = pl.BlockSpec((None, tm, N_HEADS, HEAD_DIM), lambda b, i: (layer, b * nt + i, 0, 0))
    in_specs = [pl.BlockSpec(memory_space=pltpu.SMEM), row_spec, pl.BlockSpec((d, 3 * d), lambda b, i: (0, 0))]
    args = [slopes, x, w]
    aliases = {}
    if kv_prev is not None:
        in_specs += [pl.BlockSpec(memory_space=pl.ANY)] * 2
        args += list(kv_prev)
        aliases = {3: 3, 4: 4}
    return pl.pallas_call(
        functools.partial(_qkv_prompt_kernel, len(aliases)),
        grid=(nb_batch, nt),
        in_specs=in_specs,
        out_specs=[hm2_spec, hm_spec, hm2_spec, kv_spec, kv_spec],
        out_shape=[hm2, hm, hm2, kv, kv],
        scratch_shapes=[pltpu.VMEM((seq // BLOCK, d), F32)],
        input_output_aliases=aliases,
        compiler_params=_params("parallel", "arbitrary"),
        name="qkv_prompt",
    )(*args)


def _mm_kernel(x_ref, w_ref, o_ref):
    o_ref[...] = jnp.dot(x_ref[...].astype(BF16), w_ref[...], preferred_element_type=F32)


def _mm(x, w, tn=1024):
    m, k = x.shape
    n = w.shape[1]
    assert n % tn == 0
    return pl.pallas_call(
        _mm_kernel,
        grid=(n // tn,),
        in_specs=[pl.BlockSpec((m, k), lambda j: (0, 0)), pl.BlockSpec((k, tn), lambda j: (0, j))],
        out_specs=pl.BlockSpec((m, tn), lambda j: (0, j)),
        out_shape=jax.ShapeDtypeStruct((m, n), F32),
        compiler_params=_params("parallel"),
        name="mm_rows",
    )(x, w)


def _proj_ln_kernel(has_bias, a_ref, w_ref, *refs):
    if has_bias:
        bias_ref, y_ref, g_ref, b_ref, o_ref = refs
    else:
        y_ref, g_ref, b_ref, o_ref = refs
    mp = jnp.dot(a_ref[...].astype(BF16), w_ref[...], preferred_element_type=F32)
    if has_bias:
        mp = mp + bias_ref[...]
    o_ref[...] = _layer_norm(ALPHA * y_ref[...] + mp, g_ref[...], b_ref[...])


def _proj_ln(a, w, bias, y, g, b):
    m, d = y.shape
    tm = min(TM_ROWS, m)
    assert m % tm == 0
    row = lambda i: (i, 0)
    fixed = lambda i: (0, 0)
    vec = pl.BlockSpec((1, d), fixed)
    in_specs = [pl.BlockSpec((tm, a.shape[1]), row), pl.BlockSpec(w.shape, fixed)]
    args = [a, w]
    if bias is not None:
        in_specs.append(vec)
        args.append(bias.reshape(1, d))
    in_specs += [pl.BlockSpec((tm, d), row), vec, vec]
    args += [y, g.reshape(1, d), b.reshape(1, d)]
    return pl.pallas_call(
        functools.partial(_proj_ln_kernel, bias is not None),
        grid=(m // tm,),
        in_specs=in_specs,
        out_specs=pl.BlockSpec((tm, d), row),
        out_shape=jax.ShapeDtypeStruct((m, d), F32),
        compiler_params=_params("parallel"),
        name="proj_ln",
    )(*args)


def _pw1_glu_kernel(x_ref, w_ref, b_ref, u_ref):
    d = u_ref.shape[1]
    xb = x_ref[...].astype(BF16)
    a = jnp.dot(xb, w_ref[:, 0:d], preferred_element_type=F32) + b_ref[:, 0:d]
    gt = jnp.dot(xb, w_ref[:, d:2 * d], preferred_element_type=F32) + b_ref[:, d:2 * d]
    u_ref[...] = a * jax.nn.sigmoid(gt)


def _pw1_glu(x, w, bias):
    m, d = x.shape
    tm = min(TM_ROWS, m)
    assert m % tm == 0
    return pl.pallas_call(
        _pw1_glu_kernel,
        grid=(m // tm,),
        in_specs=[pl.BlockSpec((tm, d), lambda i: (i, 0)), pl.BlockSpec((d, 2 * d), lambda i: (0, 0)),
                  pl.BlockSpec((1, 2 * d), lambda i: (0, 0))],
        out_specs=pl.BlockSpec((tm, d), lambda i: (i, 0)),
        out_shape=jax.ShapeDtypeStruct((m, d), F32),
        compiler_params=_params("parallel"),
        name="pw1_glu",
    )(x, w, bias.reshape(1, 2 * d))


def _dwconv_ln_prompt_kernel(tiles_per_seq, u_ref, halo_ref, w_ref, b_ref, g_ref, be_ref, o_ref, ext_ref, y_ref):
    i = pl.program_id(0)
    tm, d = u_ref.shape
    first = (i % tiles_per_seq) == 0
    ext_ref[0:CONV_HALO, :] = jnp.where(first, 0.0, halo_ref[...])
    ext_ref[CONV_HALO:CONV_HALO + tm, :] = u_ref[...]
    off = CONV_HALO - (CONV_W - 1)

    def row_body(r, carry):
        r0 = pl.multiple_of(r * CONV_R, CONV_R)
        for c in range(d // CONV_C):
            cs = slice(c * CONV_C, (c + 1) * CONV_C)
            win = ext_ref[pl.ds(r0, CONV_R + CONV_HALO), cs]
            acc = jnp.zeros((CONV_R, CONV_C), F32)
            for ph in range(SUBLANES):
                taps = [k for k in range(CONV_W) if (off + k) % SUBLANES == ph]
                if not taps:
                    continue
                wp = win if ph == 0 else pltpu.roll(win, CONV_R + CONV_HALO - ph, 0)
                for k in taps:
                    a = off + k - ph
                    acc = acc + wp[a:a + CONV_R, :] * w_ref[k:k + 1, cs]
            y_ref[pl.ds(r0, CONV_R), cs] = acc + b_ref[:, cs]
        return carry

    lax.fori_loop(0, tm // CONV_R, row_body, 0)
    z = _layer_norm(y_ref[...], g_ref[...], be_ref[...])
    o_ref[...] = _silu(z).astype(BF16)


def _dwconv_ln_prompt(u, w, bias, g, b, seq):
    m, d = u.shape
    tm = TM_CONV
    assert seq % tm == 0 and tm % CONV_R == 0 and tm % CONV_HALO == 0 and d % CONV_C == 0
    assert CONV_HALO >= CONV_W - 1 and CONV_HALO % SUBLANES == 0
    hb = tm // CONV_HALO
    vec = pl.BlockSpec((1, d), lambda i: (0, 0))
    return pl.pallas_call(
        functools.partial(_dwconv_ln_prompt_kernel, seq // tm),
        grid=(m // tm,),
        in_specs=[pl.BlockSpec((tm, d), lambda i: (i, 0)),
                  pl.BlockSpec((CONV_HALO, d), lambda i: (jnp.maximum(i * hb - 1, 0), 0)),
                  pl.BlockSpec((CONV_W, d), lambda i: (0, 0)), vec, vec, vec],
        out_specs=pl.BlockSpec((tm, d), lambda i: (i, 0)),
        out_shape=jax.ShapeDtypeStruct((m, d), BF16),
        scratch_shapes=[pltpu.VMEM((tm + CONV_HALO, d), F32), pltpu.VMEM((tm, d), F32)],
        compiler_params=_params("parallel"),
        name="dwconv_ln_prompt",
    )(u, u, w, bias.reshape(1, d), g.reshape(1, d), b.reshape(1, d))


def _dwconv_ln_sample_kernel(st_ref, u_ref, w_ref, b_ref, g_ref, be_ref, o_ref):
    st = st_ref[...]
    w = w_ref[...]
    y = jnp.sum(st * w[None, 0:CONV_W - 1, :], axis=1)
    y = y + u_ref[...] * w[CONV_W - 1:CONV_W, :] + b_ref[...]
    z = _layer_norm(y, g_ref[...], be_ref[...])
    o_ref[...] = _silu(z).astype(BF16)


def _dwconv_ln_sample(state, u, w, bias, g, b):
    nb, hist, d = state.shape
    tb = min(TB_CONV_S, nb)
    assert nb % tb == 0 and hist == CONV_W - 1
    vec = pl.BlockSpec((1, d), lambda i: (0, 0))
    return pl.pallas_call(
        _dwconv_ln_sample_kernel,
        grid=(nb // tb,),
        in_specs=[pl.BlockSpec((tb, hist, d), lambda i: (i, 0, 0)), pl.BlockSpec((tb, d), lambda i: (i, 0)),
                  pl.BlockSpec((CONV_W, d), lambda i: (0, 0)), vec, vec, vec],
        out_specs=pl.BlockSpec((tb, d), lambda i: (i, 0)),
        out_shape=jax.ShapeDtypeStruct((nb, d), BF16),
        compiler_params=_params("parallel"),
        name="dwconv_ln_sample",
    )(state, u, w, bias.reshape(1, d), g.reshape(1, d), b.reshape(1, d))


def _aug_lanes(nb):
    return dict(tq=nb, r=nb + 1, kb=nb + 2, qb=nb + 3)


def _query_operand(q, qs, km, slope, first_block, nb, eye):
    rows = q.shape[0]
    ln = _aug_lanes(nb)
    gate = lax.dot_general(km.astype(BF16), q.astype(BF16), _NT, preferred_element_type=F32)
    blk = lax.broadcasted_iota(jnp.int32, (nb, rows), 0)
    own_t = first_block + lax.broadcasted_iota(jnp.int32, (nb, rows), 1) // BLOCK
    past = blk < own_t
    gw = jnp.where(past, gate, -jnp.inf)
    valid_bias = jnp.where(past, 0.0, NEG)
    sel_t = jnp.full((nb, rows), NEG, F32)
    blkf = blk.astype(F32)
    for _ in range(TOPK):
        mx = jnp.max(gw, axis=0, keepdims=True)
        idx = jnp.min(jnp.where(gw == mx, blkf, float(nb)), axis=0, keepdims=True)
        pick = blkf == idx
        sel_t = jnp.where(pick, valid_bias, sel_t)
        gw = jnp.where(pick, -jnp.inf, gw)
    sel_t = jnp.where(blk == own_t, 0.0, sel_t)
    sel_p = jnp.concatenate([sel_t.astype(BF16), jnp.zeros((HEAD_DIM - nb, rows), BF16)], axis=0)
    selbias = jnp.concatenate(
        [lax.dot_general(eye, sel_p[:, a * BLOCK:(a + 1) * BLOCK], _NT, preferred_element_type=F32)
         for a in range(rows // BLOCK)], axis=0)

    lane = lax.broadcasted_iota(jnp.int32, (rows, HEAD_DIM), 1)
    rowi = lax.broadcasted_iota(jnp.int32, (rows, HEAD_DIM), 0)
    own = first_block + rowi // BLOCK
    aug = jnp.where(lane < nb, selbias, 0.0)
    aug = jnp.where(lane == ln["tq"], -slope * (rowi % BLOCK).astype(F32), aug)
    aug = jnp.where((lane == ln["r"]) | (lane == ln["kb"]), 1.0, aug)
    aug = jnp.where(lane == ln["qb"], -slope * (BLOCK * own).astype(F32), aug)
    return jnp.concatenate([qs.astype(BF16), aug.astype(BF16)], axis=1)


def _moba_prompt_kernel(slopes_ref, qa_ref, k_ref, v_ref, o_ref, tab_ref, acc_ref):
    h = pl.program_id(1)
    i = pl.program_id(2)
    qt = qa_ref.shape[0]
    nsub = qt // BLOCK
    nb = k_ref.shape[0] // BLOCK
    ln = _aug_lanes(nb)
    slope = slopes_ref[h]

    @pl.when(i == 0)
    def _build_key_table():
        lane_b = lax.broadcasted_iota(jnp.int32, (BLOCK, HEAD_DIM), 1)
        rowf_b = lax.broadcasted_iota(jnp.int32, (BLOCK, HEAD_DIM), 0).astype(F32)
        base = jnp.where((lane_b == ln["tq"]) | (lane_b == ln["qb"]), 1.0, 0.0)
        base = jnp.where(lane_b == ln["r"], slope * rowf_b, base)

        def tab_body(b, carry):
            t = jnp.where(lane_b == b, 1.0, base)
            t = jnp.where(lane_b == ln["kb"], slope * jnp.asarray(BLOCK * b, F32), t)
            tab_ref[pl.ds(pl.multiple_of(b * BLOCK, BLOCK), BLOCK), :] = t.astype(BF16)
            return carry

        lax.fori_loop(0, nb, tab_body, 0)

    qas = [qa_ref[a * BLOCK:(a + 1) * BLOCK, :] for a in range(nsub)]

    def key_tile(j):
        rows = pl.ds(pl.multiple_of(j * MOBA_KT, MOBA_KT), MOBA_KT)
        return jnp.concatenate([k_ref[rows, :], tab_ref[rows, :]], axis=1), v_ref[rows, :]

    ka, vb = key_tile(i)
    ms = []
    for a in range(nsub):
        nk = (a + 1) * BLOCK
        s = lax.dot_general(qas[a], ka[0:nk, :], _NT, preferred_element_type=F32)
        diff = (lax.broadcasted_iota(jnp.int32, (BLOCK, nk), 1)
                - lax.broadcasted_iota(jnp.int32, (BLOCK, nk), 0))
        s = jnp.where(diff <= a * BLOCK, s, NEG)
        m0 = jnp.max(s, axis=1, keepdims=True)
        ms.append(m0)
        acc_ref[a] = jnp.dot(jnp.exp(s - m0).astype(BF16), vb[0:nk, :], preferred_element_type=F32)

    def tile_body(j, ms):
        ka, vb = key_tile(j)
        out = []
        for a in range(nsub):
            s = lax.dot_general(qas[a], ka, _NT, preferred_element_type=F32)
            m_new = jnp.maximum(ms[a], jnp.max(s, axis=1, keepdims=True))
            out.append(m_new)
            acc_ref[a] = (jnp.exp(ms[a] - m_new) * acc_ref[a]
                          + jnp.dot(jnp.exp(s - m_new).astype(BF16), vb, preferred_element_type=F32))
        return tuple(out)

    lax.fori_loop(0, i, tile_body, tuple(ms))
    for a in range(nsub):
        acc = acc_ref[a]
        o_ref[a * BLOCK:(a + 1) * BLOCK, :] = (acc[:, 0:HEAD_DIM] / acc[:, HEAD_DIM:2 * HEAD_DIM]).astype(BF16)


def _moba_prompt(qa, k, v, slopes):
    nbt, nh, seq, hd = k.shape
    qt = MOBA_QT
    assert seq % qt == 0 and MOBA_KT == qt and qt % BLOCK == 0
    assert hd == HEAD_DIM == LANES and nh == N_HEADS and v.shape[-1] == 2 * hd and qa.shape[-1] == 2 * hd
    nq = seq // qt
    assert seq // BLOCK + 4 <= HEAD_DIM
    qspec = pl.BlockSpec((None, None, qt, 2 * hd), lambda b, h, i: (b, h, i, 0))
    kspec = pl.BlockSpec((None, None, seq, hd), lambda b, h, i: (b, h, 0, 0))
    vspec = pl.BlockSpec((None, None, seq, 2 * hd), lambda b, h, i: (b, h, 0, 0))
    return pl.pallas_call(
        _moba_prompt_kernel,
        grid=(nbt, nh, nq),
        in_specs=[pl.BlockSpec(memory_space=pltpu.SMEM), qspec, kspec, vspec],
        out_specs=pl.BlockSpec((qt, hd), lambda b, h, i: (b * nq + i, h)),
        out_shape=jax.ShapeDtypeStruct((nbt * seq, nh * hd), BF16),
        scratch_shapes=[pltpu.VMEM((seq, hd), BF16), pltpu.VMEM((qt // BLOCK, BLOCK, 2 * hd), F32)],
        compiler_params=_params("parallel", "parallel", "arbitrary"),
        name="moba_prompt",
    )(slopes, qa, k, v)


def _moba_sample_kernel(n_pages, pt_ref, slopes_ref, q_ref, kn_ref, vn_ref, *refs):
    del pt_ref
    k_refs = refs[:n_pages]
    v_refs = refs[n_pages:2 * n_pages]
    o_ref = refs[2 * n_pages]
    base_ref = refs[2 * n_pages + 1]
    page = k_refs[0].shape[0]
    past_len = n_pages * page
    nbp = past_len // BLOCK
    ppb = BLOCK // page
    lpp = page * N_HEADS
    lpb = BLOCK * N_HEADS
    lane_b = lax.broadcasted_iota(jnp.int32, (N_HEADS, lpb), 1)
    row_b = lax.broadcasted_iota(jnp.int32, (N_HEADS, lpb), 0)
    own_head = (lane_b % N_HEADS) == row_b

    @pl.when(pl.program_id(0) == 0)
    def _build_bias():
        slopes = slopes_ref[...]
        for b in range(nbp):
            dist = (past_len - b * BLOCK - lane_b // N_HEADS).astype(F32)
            base_ref[:, b * lpb:(b + 1) * lpb] = jnp.where(own_head, -slopes * dist, NEG)

    q8 = q_ref[...]
    qb = q8.astype(BF16)

    raws = []
    for b in range(nbp):
        parts = []
        for pg in range(b * ppb, (b + 1) * ppb):
            kf = k_refs[pg][...].reshape(lpp, HEAD_DIM).astype(BF16)
            parts.append(lax.dot_general(qb, kf, _NT, preferred_element_type=F32))
        raws.append(jnp.concatenate(parts, axis=1))

    lane = lax.broadcasted_iota(jnp.int32, (N_HEADS, LANES), 1)
    gw = jnp.full((N_HEADS, LANES), -jnp.inf, F32)
    for b in range(nbp):
        gb = jnp.sum(jnp.where(own_head, raws[b], 0.0), axis=1, keepdims=True) * (1.0 / BLOCK)
        gw = jnp.where(lane == b, gb, gw)
    sel = jnp.zeros((N_HEADS, LANES), F32)
    lanef = lane.astype(F32)
    for _ in range(min(TOPK, nbp)):
        mx = jnp.max(gw, axis=1, keepdims=True)
        idx = jnp.min(jnp.where(gw == mx, lanef, float(LANES)), axis=1, keepdims=True)
        pick = lanef == idx
        sel = jnp.where(pick, 1.0, sel)
        gw = jnp.where(pick, -jnp.inf, gw)

    scores = [jnp.where(sel[:, b:b + 1] > 0.5, raws[b] * SCALE + base_ref[:, b * lpb:(b + 1) * lpb], NEG)
              for b in range(nbp)]
    s_new = jnp.sum(q8 * kn_ref[...], axis=1, keepdims=True) * SCALE

    m = s_new
    for b in range(nbp):
        m = jnp.maximum(m, jnp.max(scores[b], axis=1, keepdims=True))
    p_new = jnp.exp(s_new - m)
    l = p_new
    acc = p_new * vn_ref[...]
    for b in range(nbp):
        p = jnp.exp(scores[b] - m)
        l = l + jnp.sum(p, axis=1, keepdims=True)
        pb = p.astype(BF16)
        for t in range(ppb):
            vf = v_refs[b * ppb + t][...].reshape(lpp, HEAD_DIM).astype(BF16)
            acc = acc + jnp.dot(pb[:, t * lpp:(t + 1) * lpp], vf, preferred_element_type=F32)
    o_ref[...] = acc / l


def _moba_sample(q, k_new, v_new, cache_k, cache_v, page_table, layer, slopes):
    nb, d = q.shape
    n_pages = page_table.shape[1]
    page = cache_k.shape[2]
    assert BLOCK % page == 0 and (n_pages * page) % BLOCK == 0 and d == D_MODEL
    assert cache_k.shape[3:] == (N_HEADS, HEAD_DIM) and N_HEADS == SUBLANES
    row = pl.BlockSpec((None, N_HEADS, HEAD_DIM), lambda b, pt: (b, 0, 0))

    def page_spec(p):
        return pl.BlockSpec((None, None, page, N_HEADS, HEAD_DIM), lambda b, pt: (layer, pt[b, p], 0, 0, 0))

    heads = lambda t: t.reshape(nb, N_HEADS, HEAD_DIM)
    out = pl.pallas_call(
        functools.partial(_moba_sample_kernel, n_pages),
        grid_spec=pltpu.PrefetchScalarGridSpec(
            num_scalar_prefetch=1,
            grid=(nb,),
            in_specs=[pl.BlockSpec((N_HEADS, 1), lambda b, pt: (0, 0)), row, row, row]
                     + [page_spec(p) for p in range(n_pages)] * 2,
            out_specs=row,
            scratch_shapes=[pltpu.VMEM((N_HEADS, n_pages * page * N_HEADS), F32)],
        ),
        out_shape=jax.ShapeDtypeStruct((nb, N_HEADS, HEAD_DIM), F32),
        compiler_params=_params("arbitrary"),
        name="moba_sample",
    )(page_table, slopes.reshape(N_HEADS, 1), heads(q), heads(k_new), heads(v_new),
      *([cache_k] * n_pages), *([cache_v] * n_pages))
    return out.reshape(nb, d)


def _ffn_prompt_kernel(tiles_per_seq, x_ref, wg_ref, wu_ref, fwg_ref, fwu_ref, fbg_ref, fbu_ref, wd_ref,
                       g_ref, b_ref, y_ref, st_ref, xb_ref, acc_ref, hg_ref, hu_ref, act_ref, cg_ref, cu_ref):
    i = pl.program_id(0)
    c = pl.program_id(1)
    tm = x_ref.shape[0]
    fc = wg_ref.shape[1]
    first = (i % tiles_per_seq) == 0
    hist = FFN_CONV_W - 1

    @pl.when(c == 0)
    def _():
        xb_ref[...] = x_ref[...].astype(BF16)
        acc_ref[...] = jnp.zeros_like(acc_ref)

    for h_ref, carry_ref in ((hg_ref, cg_ref), (hu_ref, cu_ref)):
        @pl.when(first)
        def _():
            h_ref[0:SUBLANES, :] = jnp.zeros((SUBLANES, fc), F32)

        @pl.when(jnp.logical_not(first))
        def _():
            h_ref[0:SUBLANES, :] = carry_ref[c]

    def conv(h_ref, fw_ref, fb_ref, r0, cs):
        win = h_ref[r0:r0 + FFN_R + SUBLANES, cs]
        cur = slice(SUBLANES, SUBLANES + FFN_R)
        return (fw_ref[0:1, cs] * pltpu.roll(win, 2, 0)[cur, :]
                + fw_ref[1:2, cs] * pltpu.roll(win, 1, 0)[cur, :]
                + fw_ref[2:3, cs] * win[cur, :] + fb_ref[:, cs])

    sr = tm // FFN_SUB
    for sub in range(FFN_SUB):
        rs = slice(sub * sr, (sub + 1) * sr)
        hs = slice(SUBLANES + sub * sr, SUBLANES + (sub + 1) * sr)
        xs = xb_ref[rs, :]
        hg_ref[hs, :] = jnp.dot(xs, wg_ref[...], preferred_element_type=F32)
        hu_ref[hs, :] = jnp.dot(xs, wu_ref[...], preferred_element_type=F32)
        for r0 in range(sub * sr, (sub + 1) * sr, FFN_R):
            for cc in range(fc // FFN_C):
                cs = slice(cc * FFN_C, (cc + 1) * FFN_C)
                gt = conv(hg_ref, fwg_ref, fbg_ref, r0, cs)
                u = conv(hu_ref, fwu_ref, fbu_ref, r0, cs)
                act_ref[r0:r0 + FFN_R, cs] = (_silu(gt) * u).astype(BF16)
        acc = acc_ref[rs, :] + jnp.dot(act_ref[rs, :], wd_ref[...], preferred_element_type=F32)
        acc_ref[rs, :] = acc
        y_ref[rs, :] = _layer_norm(ALPHA * x_ref[rs, :] + acc, g_ref[...], b_ref[...])

    cg_ref[c] = hg_ref[tm:tm + SUBLANES, :]
    cu_ref[c] = hu_ref[tm:tm + SUBLANES, :]
    for r in range(hist):
        row = SUBLANES + tm - hist + r
        st_ref[r] = jnp.concatenate([hg_ref[row:row + 1, :], hu_ref[row:row + 1, :]], axis=0)


def _ffn_prompt(x, w_up, w_fdw, b_fdw, w_down, g, b, nb_batch, seq):
    m, d = x.shape
    f = w_down.shape[0]
    tm = TM_FFN
    fc = FC_FFN if f % FC_FFN == 0 else FFN_C
    assert seq % tm == 0 and f % fc == 0 and FFN_CONV_W == 3 and tm % FFN_R == 0 and fc % FFN_C == 0
    nc = f // fc
    tps = seq // tm
    vec = pl.BlockSpec((1, d), lambda i, c: (0, 0))
    y, st = pl.pallas_call(
        functools.partial(_ffn_prompt_kernel, tps),
        grid=(m // tm, nc),
        in_specs=[pl.BlockSpec((tm, d), lambda i, c: (i, 0)),
                  pl.BlockSpec((d, fc), lambda i, c: (0, c)), pl.BlockSpec((d, fc), lambda i, c: (0, nc + c)),
                  pl.BlockSpec((FFN_CONV_W, fc), lambda i, c: (0, c)),
                  pl.BlockSpec((FFN_CONV_W, fc), lambda i, c: (0, nc + c)),
                  pl.BlockSpec((1, fc), lambda i, c: (0, c)), pl.BlockSpec((1, fc), lambda i, c: (0, nc + c)),
                  pl.BlockSpec((fc, d), lambda i, c: (c, 0)), vec, vec],
        out_specs=[pl.BlockSpec((tm, d), lambda i, c: (i, 0)),
                   pl.BlockSpec((None, FFN_CONV_W - 1, 2, fc), lambda i, c: (i, 0, 0, c))],
        out_shape=[jax.ShapeDtypeStruct((m, d), F32),
                   jax.ShapeDtypeStruct((m // tm, FFN_CONV_W - 1, 2, f), F32)],
        scratch_shapes=[pltpu.VMEM((tm, d), BF16), pltpu.VMEM((tm, d), F32),
                        pltpu.VMEM((tm + SUBLANES, fc), F32), pltpu.VMEM((tm + SUBLANES, fc), F32),
                        pltpu.VMEM((tm, fc), BF16),
                        pltpu.VMEM((nc, SUBLANES, fc), F32), pltpu.VMEM((nc, SUBLANES, fc), F32)],
        compiler_params=_params("arbitrary", "arbitrary"),
        name="ffn_prompt",
    )(x, w_up, w_up, w_fdw, w_fdw, b_fdw.reshape(1, 2 * f), b_fdw.reshape(1, 2 * f), w_down,
      g.reshape(1, d), b.reshape(1, d))
    return y, st[tps - 1::tps].reshape(nb_batch, FFN_CONV_W - 1, 2 * f)


def _ffn_sample_kernel(x_ref, s0g_ref, s0u_ref, s1g_ref, s1u_ref, wg_ref, wu_ref, fwg_ref, fwu_ref,
                       fbg_ref, fbu_ref, wd_ref, g_ref, b_ref, y_ref, hg_ref, hu_ref, acc_ref):
    c = pl.program_id(0)

    @pl.when(c == 0)
    def _():
        acc_ref[...] = jnp.zeros_like(acc_ref)

    xb = x_ref[...].astype(BF16)
    hg = jnp.dot(xb, wg_ref[...], preferred_element_type=F32)
    hu = jnp.dot(xb, wu_ref[...], preferred_element_type=F32)
    hg_ref[...] = hg
    hu_ref[...] = hu
    gt = fwg_ref[0:1, :] * s0g_ref[...] + fwg_ref[1:2, :] * s1g_ref[...] + fwg_ref[2:3, :] * hg + fbg_ref[...]
    u = fwu_ref[0:1, :] * s0u_ref[...] + fwu_ref[1:2, :] * s1u_ref[...] + fwu_ref[2:3, :] * hu + fbu_ref[...]
    act = (_silu(gt) * u).astype(BF16)
    acc_ref[...] += jnp.dot(act, wd_ref[...], preferred_element_type=F32)

    @pl.when(c == pl.num_programs(0) - 1)
    def _():
        y_ref[...] = _layer_norm(ALPHA * x_ref[...] + acc_ref[...], g_ref[...], b_ref[...])


def _ffn_sample(x, state, w_up, w_fdw, b_fdw, w_down, g, b):
    nb, d = x.shape
    f = w_down.shape[0]
    fc = FC_FFN_S
    assert f % fc == 0 and FFN_CONV_W == 3 and state.shape == (nb, FFN_CONV_W - 1, 2 * f)
    nc = f // fc
    st = state.reshape(nb, 4 * f)
    vec = pl.BlockSpec((1, d), lambda c: (0, 0))
    col = lambda k: pl.BlockSpec((nb, fc), lambda c: (0, k * nc + c))
    y, hg, hu = pl.pallas_call(
        _ffn_sample_kernel,
        grid=(nc,),
        in_specs=[pl.BlockSpec((nb, d), lambda c: (0, 0)), col(0), col(1), col(2), col(3),
                  pl.BlockSpec((d, fc), lambda c: (0, c)), pl.BlockSpec((d, fc), lambda c: (0, nc + c)),
                  pl.BlockSpec((FFN_CONV_W, fc), lambda c: (0, c)),
                  pl.BlockSpec((FFN_CONV_W, fc), lambda c: (0, nc + c)),
                  pl.BlockSpec((1, fc), lambda c: (0, c)), pl.BlockSpec((1, fc), lambda c: (0, nc + c)),
                  pl.BlockSpec((fc, d), lambda c: (c, 0)), vec, vec],
        out_specs=[pl.BlockSpec((nb, d), lambda c: (0, 0)), pl.BlockSpec((nb, fc), lambda c: (0, c)),
                   pl.BlockSpec((nb, fc), lambda c: (0, c))],
        out_shape=[jax.ShapeDtypeStruct((nb, d), F32), jax.ShapeDtypeStruct((nb, f), F32),
                   jax.ShapeDtypeStruct((nb, f), F32)],
        scratch_shapes=[pltpu.VMEM((nb, d), F32)],
        compiler_params=_params("arbitrary"),
        name="ffn_sample",
    )(x, st, st, st, st, w_up, w_up, w_fdw, w_fdw, b_fdw.reshape(1, 2 * f), b_fdw.reshape(1, 2 * f), w_down,
      g.reshape(1, d), b.reshape(1, d))
    new_state = jnp.stack([state[:, 1, :], jnp.concatenate([hg, hu], axis=1)], axis=1)
    return y, new_state


def kernel(x_prompt, x_sample, cache_k, cache_v, state_conv, state_ffn, page_table, w_qkv, w_o, w_pw1, b_pw1,
           w_dw, b_dw, ln_cv_g, ln_cv_b, w_pw2, b_pw2, w_up, w_fdw, b_fdw, w_down, ln1_g, ln1_b, ln2_g, ln2_b):
    bp, sp, d = x_prompt.shape
    bs, ss, _ = x_sample.shape
    assert d == D_MODEL and ss == 1
    slopes = jnp.asarray(_alibi_slopes())
    yp = x_prompt.reshape(bp * sp, d)
    ys = x_sample.reshape(bs, d)
    k_sa, v_sa, c_pr, c_sa, f_pr, f_sa = [], [], [], [], [], []
    kv_pr = None
    kv_shape = ((DEPTH + 1) // 2, bp, sp, N_HEADS, HEAD_DIM)
    for i in range(DEPTH):
        j = i // 2
        if i % 2 == 0:
            wq = w_qkv[j].astype(BF16)
            wo = w_o[j].astype(BF16)
            qa, k, v, kf, vf = _qkv_prompt(yp, wq, bp, sp, j, kv_pr, slopes)
            kv_pr = (kf, vf)
            op = _moba_prompt(qa, k, v, slopes)
            qkv_s = _mm(ys, wq)
            q_s, k_s, v_s = qkv_s[:, 0:d], qkv_s[:, d:2 * d], qkv_s[:, 2 * d:3 * d]
            os_ = _moba_sample(q_s, k_s, v_s, cache_k, cache_v, page_table, j, slopes)
            yp = _proj_ln(op, wo, None, yp, ln1_g[i], ln1_b[i])
            ys = _proj_ln(os_, wo, None, ys, ln1_g[i], ln1_b[i])
            k_sa.append(k_s.reshape(bs, ss, N_HEADS, HEAD_DIM))
            v_sa.append(v_s.reshape(bs, ss, N_HEADS, HEAD_DIM))
        else:
            w1 = w_pw1[j].astype(BF16)
            w2 = w_pw2[j].astype(BF16)
            up = _pw1_glu(yp, w1, b_pw1[j])
            zp = _dwconv_ln_prompt(up, w_dw[j], b_dw[j], ln_cv_g[j], ln_cv_b[j], sp)
            us = _pw1_glu(ys, w1, b_pw1[j])
            zs = _dwconv_ln_sample(state_conv[j], us, w_dw[j], b_dw[j], ln_cv_g[j], ln_cv_b[j])
            yp = _proj_ln(zp, w2, b_pw2[j], yp, ln1_g[i], ln1_b[i])
            ys = _proj_ln(zs, w2, b_pw2[j], ys, ln1_g[i], ln1_b[i])
            c_pr.append(up.reshape(bp, sp, d)[:, sp - (CONV_W - 1):, :])
            c_sa.append(jnp.concatenate([state_conv[j][:, 1:, :], us[:, None, :]], axis=1))
        wu = w_up[i].astype(BF16)
        wd = w_down[i].astype(BF16)
        yp, fp = _ffn_prompt(yp, wu, w_fdw[i], b_fdw[i], wd, ln2_g[i], ln2_b[i], bp, sp)
        ys, fs = _ffn_sample(ys, state_ffn[i], wu, w_fdw[i], b_fdw[i], wd, ln2_g[i], ln2_b[i])
        f_pr.append(fp)
        f_sa.append(fs)
    return (yp.reshape(bp, sp, d), ys.reshape(bs, ss, d), kv_pr[0].reshape(kv_shape), kv_pr[1].reshape(kv_shape),
            jnp.stack(k_sa),
            jnp.stack(v_sa), jnp.stack(c_pr), jnp.stack(c_sa), jnp.stack(f_pr), jnp.stack(f_sa))
```

```python
import functools
import math

import numpy as np
import jax
import jax.numpy as jnp
from jax import lax
from jax.experimental import pallas as pl
from jax.experimental.pallas import tpu as pltpu

N_HEADS = 8
HEAD_DIM = 128
D_MODEL = N_HEADS * HEAD_DIM
BLOCK = 256
TOPK = 3
CONV_W = 31
FFN_CONV_W = 3
DEPTH = 4
ALPHA = (2.0 * DEPTH) ** 0.25
LN_EPS = 1e-5
NEG = -1e30
SCALE = 1.0 / math.sqrt(HEAD_DIM)

LANES = 128
SUBLANES = 8
VMEM_LIMIT = 56 * 1024 * 1024

F32 = jnp.float32
BF16 = jnp.bfloat16

TM_ROWS = 512
TM_CONV = 256
CONV_HALO = 32
CONV_R = 64
CONV_C = 128
MOBA_QT = 1024
MOBA_KT = 1024
TM_FFN = 512
FC_FFN = 1408
FFN_SUB = 2
FFN_R = 32
FFN_C = 128
TB_CONV_S = 32

_NT = (((1,), (1,)), ((), ()))


def _alibi_slopes():
    s = np.exp2(-8.0 * (np.arange(N_HEADS, dtype=np.float64) + 1.0) / N_HEADS).astype(np.float32)
    assert np.all(s.astype(jnp.bfloat16).astype(np.float32) == s)
    assert np.all(np.log2(s) == np.round(np.log2(s)))
    return s


def _params(*sem):
    return pltpu.CompilerParams(dimension_semantics=sem, vmem_limit_bytes=VMEM_LIMIT)


def _layer_norm(x, g, b):
    mu = jnp.mean(x, axis=-1, keepdims=True)
    xc = x - mu
    var = jnp.mean(xc * xc, axis=-1, keepdims=True)
    return xc * lax.rsqrt(var + LN_EPS) * g + b


def _silu(x):
    return x * jax.nn.sigmoid(x)


def _qkv_prompt_kernel(n_aliased, slopes_ref, x_ref, w_ref, *refs):
    qa_ref, k_ref, v_ref, kf_ref, vf_ref, kmt_ref = refs[n_aliased:]
    i = pl.program_id(1)
    tm = x_ref.shape[0]
    nb = kmt_ref.shape[0]
    bpt = tm // BLOCK
    xb = x_ref[...].astype(BF16)
    k = jnp.dot(xb, w_ref[:, D_MODEL:2 * D_MODEL], preferred_element_type=F32)
    kf_ref[...] = pltpu.einshape("m(hd)->mhd", k, h=N_HEADS)
    for h in range(N_HEADS):
        k_ref[h] = k[:, h * HEAD_DIM:(h + 1) * HEAD_DIM].astype(BF16)

    @pl.when(i == 0)
    def _():
        kmt_ref[...] = jnp.zeros_like(kmt_ref)

    kmt = kmt_ref[...]
    blk = lax.broadcasted_iota(jnp.int32, kmt.shape, 0)
    for j in range(bpt):
        kmt = jnp.where(blk == bpt * i + j, jnp.mean(k[j * BLOCK:(j + 1) * BLOCK], axis=0, keepdims=True), kmt)
    kmt_ref[...] = kmt

    q = jnp.dot(xb, w_ref[:, 0:D_MODEL], preferred_element_type=F32)
    qs = q * SCALE
    eye = jnp.where(lax.broadcasted_iota(jnp.int32, (BLOCK, BLOCK), 0)
                    == lax.broadcasted_iota(jnp.int32, (BLOCK, BLOCK), 1), 1.0, 0.0).astype(BF16)
    for h in range(N_HEADS):
        sl = slice(h * HEAD_DIM, (h + 1) * HEAD_DIM)
        qa_ref[h] = _query_operand(q[:, sl], qs[:, sl], kmt[:, sl], slopes_ref[h], bpt * i, nb, eye)
    v = jnp.dot(xb, w_ref[:, 2 * D_MODEL:3 * D_MODEL], preferred_element_type=F32)
    vf_ref[...] = pltpu.einshape("m(hd)->mhd", v, h=N_HEADS)
    ones = jnp.ones((tm, HEAD_DIM), BF16)
    for h in range(N_HEADS):
        v_ref[h] = jnp.concatenate([v[:, h * HEAD_DIM:(h + 1) * HEAD_DIM].astype(BF16), ones], axis=1)


def _qkv_prompt(x, w, nb_batch, seq, layer, kv_prev, slopes):
    m, d = x.shape
    tm = TM_ROWS
    assert seq % tm == 0 and tm % BLOCK == 0 and d == D_MODEL and seq // BLOCK + 4 <= HEAD_DIM
    nt = seq // tm
    n_attn = (DEPTH + 1) // 2
    hm = jax.ShapeDtypeStruct((nb_batch, N_HEADS, seq, HEAD_DIM), BF16)
    hm_spec = pl.BlockSpec((None, N_HEADS, tm, HEAD_DIM), lambda b, i: (b, 0, i, 0))
    hm2 = jax.ShapeDtypeStruct((nb_batch, N_HEADS, seq, 2 * HEAD_DIM), BF16)
    hm2_spec = pl.BlockSpec((None, N_HEADS, tm, 2 * HEAD_DIM), lambda b, i: (b, 0, i, 0))
    row_spec = pl.BlockSpec((tm, d), lambda b, i: (b * nt + i, 0))
    kv = jax.ShapeDtypeStruct((n_attn, m, N_HEADS, HEAD_DIM), F32)
    kv_spec = pl.BlockSpec((None, tm, N_HEADS, HEAD_DIM), lambda b, i: (layer, b * nt + i, 0, 0))
    in_specs = [pl.BlockSpec(memory_space=pltpu.SMEM), row_spec, pl.BlockSpec((d, 3 * d), lambda b, i: (0, 0))]
    args = [slopes, x, w]
    aliases = {}
    if kv_prev is not None:
        in_specs += [pl.BlockSpec(memory_space=pl.ANY)] * 2
        args += list(kv_prev)
        aliases = {3: 3, 4: 4}
    return pl.pallas_call(
        functools.partial(_qkv_prompt_kernel, len(aliases)),
        grid=(nb_batch, nt),
        in_specs=in_specs,
        out_specs=[hm2_spec, hm_spec, hm2_spec, kv_spec, kv_spec],
        out_shape=[hm2, hm, hm2, kv, kv],
        scratch_shapes=[pltpu.VMEM((seq // BLOCK, d), F32)],
        input_output_aliases=aliases,
        compiler_params=_params("parallel", "arbitrary"),
        name="qkv_prompt",
    )(*args)


def _mm_kernel(x_ref, w_ref, o_ref):
    o_ref[...] = jnp.dot(x_ref[...].astype(BF16), w_ref[...], preferred_element_type=F32)


def _mm(x, w, tn=1024):
    m, k = x.shape
    n = w.shape[1]
    assert n % tn == 0
    return pl.pallas_call(
        _mm_kernel,
        grid=(n // tn,),
        in_specs=[pl.BlockSpec((m, k), lambda j: (0, 0)), pl.BlockSpec((k, tn), lambda j: (0, j))],
        out_specs=pl.BlockSpec((m, tn), lambda j: (0, j)),
        out_shape=jax.ShapeDtypeStruct((m, n), F32),
        compiler_params=_params("parallel"),
        name="mm_rows",
    )(x, w)


def _proj_ln_kernel(has_bias, a_ref, w_ref, *refs):
    if has_bias:
        bias_ref, y_ref, g_ref, b_ref, o_ref = refs
    else:
        y_ref, g_ref, b_ref, o_ref = refs
    mp = jnp.dot(a_ref[...].astype(BF16), w_ref[...], preferred_element_type=F32)
    if has_bias:
        mp = mp + bias_ref[...]
    o_ref[...] = _layer_norm(ALPHA * y_ref[...] + mp, g_ref[...], b_ref[...])


def _proj_ln(a, w, bias, y, g, b):
    m, d = y.shape
    tm = min(TM_ROWS, m)
    assert m % tm == 0
    row = lambda i: (i, 0)
    fixed = lambda i: (0, 0)
    vec = pl.BlockSpec((1, d), fixed)
    in_specs = [pl.BlockSpec((tm, a.shape[1]), row), pl.BlockSpec(w.shape, fixed)]
    args = [a, w]
    if bias is not None:
        in_specs.append(vec)
        args.append(bias.reshape(1, d))
    in_specs += [pl.BlockSpec((tm, d), row), vec, vec]
    args += [y, g.reshape(1, d), b.reshape(1, d)]
    return pl.pallas_call(
        functools.partial(_proj_ln_kernel, bias is not None),
        grid=(m // tm,),
        in_specs=in_specs,
        out_specs=pl.BlockSpec((tm, d), row),
        out_shape=jax.ShapeDtypeStruct((m, d), F32),
        compiler_params=_params("parallel"),
        name="proj_ln",
    )(*args)


def _pw1_glu_kernel(x_ref, w_ref, b_ref, u_ref):
    d = u_ref.shape[1]
    xb = x_ref[...].astype(BF16)
    a = jnp.dot(xb, w_ref[:, 0:d], preferred_element_type=F32) + b_ref[:, 0:d]
    gt = jnp.dot(xb, w_ref[:, d:2 * d], preferred_element_type=F32) + b_ref[:, d:2 * d]
    u_ref[...] = a * jax.nn.sigmoid(gt)


def _pw1_glu(x, w, bias):
    m, d = x.shape
    tm = min(TM_ROWS, m)
    assert m % tm == 0
    return pl.pallas_call(
        _pw1_glu_kernel,
        grid=(m // tm,),
        in_specs=[pl.BlockSpec((tm, d), lambda i: (i, 0)), pl.BlockSpec((d, 2 * d), lambda i: (0, 0)),
                  pl.BlockSpec((1, 2 * d), lambda i: (0, 0))],
        out_specs=pl.BlockSpec((tm, d), lambda i: (i, 0)),
        out_shape=jax.ShapeDtypeStruct((m, d), F32),
        compiler_params=_params("parallel"),
        name="pw1_glu",
    )(x, w, bias.reshape(1, 2 * d))


def _dwconv_ln_prompt_kernel(tiles_per_seq, u_ref, halo_ref, w_ref, b_ref, g_ref, be_ref, o_ref, ext_ref, y_ref):
    i = pl.program_id(0)
    tm, d = u_ref.shape
    first = (i % tiles_per_seq) == 0
    ext_ref[0:CONV_HALO, :] = jnp.where(first, 0.0, halo_ref[...])
    ext_ref[CONV_HALO:CONV_HALO + tm, :] = u_ref[...]
    off = CONV_HALO - (CONV_W - 1)

    def row_body(r, carry):
        r0 = pl.multiple_of(r * CONV_R, CONV_R)
        for c in range(d // CONV_C):
            cs = slice(c * CONV_C, (c + 1) * CONV_C)
            win = ext_ref[pl.ds(r0, CONV_R + CONV_HALO), cs]
            acc = jnp.zeros((CONV_R, CONV_C), F32)
            for ph in range(SUBLANES):
                taps = [k for k in range(CONV_W) if (off + k) % SUBLANES == ph]
                if not taps:
                    continue
                wp = win if ph == 0 else pltpu.roll(win, CONV_R + CONV_HALO - ph, 0)
                for k in taps:
                    a = off + k - ph
                    acc = acc + wp[a:a + CONV_R, :] * w_ref[k:k + 1, cs]
            y_ref[pl.ds(r0, CONV_R), cs] = acc + b_ref[:, cs]
        return carry

    lax.fori_loop(0, tm // CONV_R, row_body, 0)
    z = _layer_norm(y_ref[...], g_ref[...], be_ref[...])
    o_ref[...] = _silu(z).astype(BF16)


def _dwconv_ln_prompt(u, w, bias, g, b, seq):
    m, d = u.shape
    tm = TM_CONV
    assert seq % tm == 0 and tm % CONV_R == 0 and tm % CONV_HALO == 0 and d % CONV_C == 0
    assert CONV_HALO >= CONV_W - 1 and CONV_HALO % SUBLANES == 0
    hb = tm // CONV_HALO
    vec = pl.BlockSpec((1, d), lambda i: (0, 0))
    return pl.pallas_call(
        functools.partial(_dwconv_ln_prompt_kernel, seq // tm),
        grid=(m // tm,),
        in_specs=[pl.BlockSpec((tm, d), lambda i: (i, 0)),
                  pl.BlockSpec((CONV_HALO, d), lambda i: (jnp.maximum(i * hb - 1, 0), 0)),
                  pl.BlockSpec((CONV_W, d), lambda i: (0, 0)), vec, vec, vec],
        out_specs=pl.BlockSpec((tm, d), lambda i: (i, 0)),
        out_shape=jax.ShapeDtypeStruct((m, d), BF16),
        scratch_shapes=[pltpu.VMEM((tm + CONV_HALO, d), F32), pltpu.VMEM((tm, d), F32)],
        compiler_params=_params("parallel"),
        name="dwconv_ln_prompt",
    )(u, u, w, bias.reshape(1, d), g.reshape(1, d), b.reshape(1, d))


def _dwconv_ln_sample_kernel(st_ref, u_ref, w_ref, b_ref, g_ref, be_ref, o_ref):
    st = st_ref[...]
    w = w_ref[...]
    y = jnp.sum(st * w[None, 0:CONV_W - 1, :], axis=1)
    y = y + u_ref[...] * w[CONV_W - 1:CONV_W, :] + b_ref[...]
    z = _layer_norm(y, g_ref[...], be_ref[...])
    o_ref[...] = _silu(z).astype(BF16)


def _dwconv_ln_sample(state, u, w, bias, g, b):
    nb, hist, d = state.shape
    tb = min(TB_CONV_S, nb)
    assert nb % tb == 0 and hist == CONV_W - 1
    vec = pl.BlockSpec((1, d), lambda i: (0, 0))
    return pl.pallas_call(
        _dwconv_ln_sample_kernel,
        grid=(nb // tb,),
        in_specs=[pl.BlockSpec((tb, hist, d), lambda i: (i, 0, 0)), pl.BlockSpec((tb, d), lambda i: (i, 0)),
                  pl.BlockSpec((CONV_W, d), lambda i: (0, 0)), vec, vec, vec],
        out_specs=pl.BlockSpec((tb, d), lambda i: (i, 0)),
        out_shape=jax.ShapeDtypeStruct((nb, d), BF16),
        compiler_params=_params("parallel"),
        name="dwconv_ln_sample",
    )(state, u, w, bias.reshape(1, d), g.reshape(1, d), b.reshape(1, d))


def _aug_lanes(nb):
    return dict(tq=nb, r=nb + 1, kb=nb + 2, qb=nb + 3)


def _query_operand(q, qs, km, slope, first_block, nb, eye):
    rows = q.shape[0]
    ln = _aug_lanes(nb)
    gate = lax.dot_general(km.astype(BF16), q.astype(BF16), _NT, preferred_element_type=F32)
    blk = lax.broadcasted_iota(jnp.int32, (nb, rows), 0)
    own_t = first_block + lax.broadcasted_iota(jnp.int32, (nb, rows), 1) // BLOCK
    past = blk < own_t
    gw = jnp.where(past, gate, -jnp.inf)
    valid_bias = jnp.where(past, 0.0, NEG)
    sel_t = jnp.full((nb, rows), NEG, F32)
    blkf = blk.astype(F32)
    for _ in range(TOPK):
        mx = jnp.max(gw, axis=0, keepdims=True)
        idx = jnp.min(jnp.where(gw == mx, blkf, float(nb)), axis=0, keepdims=True)
        pick = blkf == idx
        sel_t = jnp.where(pick, valid_bias, sel_t)
        gw = jnp.where(pick, -jnp.inf, gw)
    sel_t = jnp.where(blk == own_t, 0.0, sel_t)
    sel_p = jnp.concatenate([sel_t.astype(BF16), jnp.zeros((HEAD_DIM - nb, rows), BF16)], axis=0)
    selbias = jnp.concatenate(
        [lax.dot_general(eye, sel_p[:, a * BLOCK:(a + 1) * BLOCK], _NT, preferred_element_type=F32)
         for a in range(rows // BLOCK)], axis=0)

    lane = lax.broadcasted_iota(jnp.int32, (rows, HEAD_DIM), 1)
    rowi = lax.broadcasted_iota(jnp.int32, (rows, HEAD_DIM), 0)
    own = first_block + rowi // BLOCK
    aug = jnp.where(lane < nb, selbias, 0.0)
    aug = jnp.where(lane == ln["tq"], -slope * (rowi % BLOCK).astype(F32), aug)
    aug = jnp.where((lane == ln["r"]) | (lane == ln["kb"]), 1.0, aug)
    aug = jnp.where(lane == ln["qb"], -slope * (BLOCK * own).astype(F32), aug)
    return jnp.concatenate([qs.astype(BF16), aug.astype(BF16)], axis=1)


def _moba_prompt_kernel(slopes_ref, qa_ref, k_ref, v_ref, o_ref, tab_ref, acc_ref):
    h = pl.program_id(1)
    i = pl.program_id(2)
    qt = qa_ref.shape[0]
    nsub = qt // BLOCK
    nb = k_ref.shape[0] // BLOCK
    ln = _aug_lanes(nb)
    slope = slopes_ref[h]

    @pl.when(i == 0)
    def _build_key_table():
        lane_b = lax.broadcasted_iota(jnp.int32, (BLOCK, HEAD_DIM), 1)
        rowf_b = lax.broadcasted_iota(jnp.int32, (BLOCK, HEAD_DIM), 0).astype(F32)
        base = jnp.where((lane_b == ln["tq"]) | (lane_b == ln["qb"]), 1.0, 0.0)
        base = jnp.where(lane_b == ln["r"], slope * rowf_b, base)

        def tab_body(b, carry):
            t = jnp.where(lane_b == b, 1.0, base)
            t = jnp.where(lane_b == ln["kb"], slope * jnp.asarray(BLOCK * b, F32), t)
            tab_ref[pl.ds(pl.multiple_of(b * BLOCK, BLOCK), BLOCK), :] = t.astype(BF16)
            return carry

        lax.fori_loop(0, nb, tab_body, 0)

    qas = [qa_ref[a * BLOCK:(a + 1) * BLOCK, :] for a in range(nsub)]

    def key_tile(j):
        rows = pl.ds(pl.multiple_of(j * MOBA_KT, MOBA_KT), MOBA_KT)
        return jnp.concatenate([k_ref[rows, :], tab_ref[rows, :]], axis=1), v_ref[rows, :]

    ka, vb = key_tile(i)
    ms = []
    for a in range(nsub):
        nk = (a + 1) * BLOCK
        s = lax.dot_general(qas[a], ka[0:nk, :], _NT, preferred_element_type=F32)
        diff = (lax.broadcasted_iota(jnp.int32, (BLOCK, nk), 1)
                - lax.broadcasted_iota(jnp.int32, (BLOCK, nk), 0))
        s = jnp.where(diff <= a * BLOCK, s, NEG)
        m0 = jnp.max(s, axis=1, keepdims=True)
        ms.append(m0)
        acc_ref[a] = jnp.dot(jnp.exp(s - m0).astype(BF16), vb[0:nk, :], preferred_element_type=F32)

    def tile_body(j, ms):
        ka, vb = key_tile(j)
        out = []
        for a in range(nsub):
            s = lax.dot_general(qas[a], ka, _NT, preferred_element_type=F32)
            m_new = jnp.maximum(ms[a], jnp.max(s, axis=1, keepdims=True))
            out.append(m_new)
            acc_ref[a] = (jnp.exp(ms[a] - m_new) * acc_ref[a]
                          + jnp.dot(jnp.exp(s - m_new).astype(BF16), vb, preferred_element_type=F32))
        return tuple(out)

    lax.fori_loop(0, i, tile_body, tuple(ms))
    for a in range(nsub):
        acc = acc_ref[a]
        o_ref[a * BLOCK:(a + 1) * BLOCK, :] = (acc[:, 0:HEAD_DIM] / acc[:, HEAD_DIM:2 * HEAD_DIM]).astype(BF16)


def _moba_prompt(qa, k, v, slopes):
    nbt, nh, seq, hd = k.shape
    qt = MOBA_QT
    assert seq % qt == 0 and MOBA_KT == qt and qt % BLOCK == 0
    assert hd == HEAD_DIM == LANES and nh == N_HEADS and v.shape[-1] == 2 * hd and qa.shape[-1] == 2 * hd
    nq = seq // qt
    assert seq // BLOCK + 4 <= HEAD_DIM
    qspec = pl.BlockSpec((None, None, qt, 2 * hd), lambda b, h, i: (b, h, i, 0))
    kspec = pl.BlockSpec((None, None, seq, hd), lambda b, h, i: (b, h, 0, 0))
    vspec = pl.BlockSpec((None, None, seq, 2 * hd), lambda b, h, i: (b, h, 0, 0))
    return pl.pallas_call(
        _moba_prompt_kernel,
        grid=(nbt, nh, nq),
        in_specs=[pl.BlockSpec(memory_space=pltpu.SMEM), qspec, kspec, vspec],
        out_specs=pl.BlockSpec((qt, hd), lambda b, h, i: (b * nq + i, h)),
        out_shape=jax.ShapeDtypeStruct((nbt * seq, nh * hd), BF16),
        scratch_shapes=[pltpu.VMEM((seq, hd), BF16), pltpu.VMEM((qt // BLOCK, BLOCK, 2 * hd), F32)],
        compiler_params=_params("parallel", "parallel", "arbitrary"),
        name="moba_prompt",
    )(slopes, qa, k, v)


def _moba_sample_kernel(n_pages, pt_ref, slopes_ref, q_ref, kn_ref, vn_ref, *refs):
    del pt_ref
    k_refs = refs[:n_pages]
    v_refs = refs[n_pages:2 * n_pages]
    o_ref = refs[2 * n_pages]
    base_ref = refs[2 * n_pages + 1]
    page = k_refs[0].shape[0]
    past_len = n_pages * page
    nbp = past_len // BLOCK
    ppb = BLOCK // page
    lpp = page * N_HEADS
    lpb = BLOCK * N_HEADS
    lane_b = lax.broadcasted_iota(jnp.int32, (N_HEADS, lpb), 1)
    row_b = lax.broadcasted_iota(jnp.int32, (N_HEADS, lpb), 0)
    own_head = (lane_b % N_HEADS) == row_b

    @pl.when(pl.program_id(0) == 0)
    def _build_bias():
        slopes = slopes_ref[...]
        for b in range(nbp):
            dist = (past_len - b * BLOCK - lane_b // N_HEADS).astype(F32)
            base_ref[:, b * lpb:(b + 1) * lpb] = jnp.where(own_head, -slopes * dist, NEG)

    q8 = q_ref[...]
    qb = q8.astype(BF16)

    raws = []
    for b in range(nbp):
        parts = []
        for pg in range(b * ppb, (b + 1) * ppb):
            kf = k_refs[pg][...].reshape(lpp, HEAD_DIM).astype(BF16)
            parts.append(lax.dot_general(qb, kf, _NT, preferred_element_type=F32))
        raws.append(jnp.concatenate(parts, axis=1))

    lane = lax.broadcasted_iota(jnp.int32, (N_HEADS, LANES), 1)
    gw = jnp.full((N_HEADS, LANES), -jnp.inf, F32)
    for b in range(nbp):
        gb = jnp.sum(jnp.where(own_head, raws[b], 0.0), axis=1, keepdims=True) * (1.0 / BLOCK)
        gw = jnp.where(lane == b, gb, gw)
    sel = jnp.zeros((N_HEADS, LANES), F32)
    lanef = lane.astype(F32)
    for _ in range(min(TOPK, nbp)):
        mx = jnp.max(gw, axis=1, keepdims=True)
        idx = jnp.min(jnp.where(gw == mx, lanef, float(LANES)), axis=1, keepdims=True)
        pick = lanef == idx
        sel = jnp.where(pick, 1.0, sel)
        gw = jnp.where(pick, -jnp.inf, gw)

    scores = [jnp.where(sel[:, b:b + 1] > 0.5, raws[b] * SCALE + base_ref[:, b * lpb:(b + 1) * lpb], NEG)
              for b in range(nbp)]
    s_new = jnp.sum(q8 * kn_ref[...], axis=1, keepdims=True) * SCALE

    m = s_new
    for b in range(nbp):
        m = jnp.maximum(m, jnp.max(scores[b], axis=1, keepdims=True))
    p_new = jnp.exp(s_new - m)
    l = p_new
    acc = p_new * vn_ref[...]
    for b in range(nbp):
        p = jnp.exp(scores[b] - m)
        l = l + jnp.sum(p, axis=1, keepdims=True)
        pb = p.astype(BF16)
        for t in range(ppb):
            vf = v_refs[b * ppb + t][...].reshape(lpp, HEAD_DIM).astype(BF16)
            acc = acc + jnp.dot(pb[:, t * lpp:(t + 1) * lpp], vf, preferred_element_type=F32)
    o_ref[...] = acc / l


def _moba_sample(q, k_new, v_new, cache_k, cache_v, page_table, layer, slopes):
    nb, d = q.shape
    n_pages = page_table.shape[1]
    page = cache_k.shape[2]
    assert BLOCK % page == 0 and (n_pages * page) % BLOCK == 0 and d == D_MODEL
    assert cache_k.shape[3:] == (N_HEADS, HEAD_DIM) and N_HEADS == SUBLANES
    row = pl.BlockSpec((None, N_HEADS, HEAD_DIM), lambda b, pt: (b, 0, 0))

    def page_spec(p):
        return pl.BlockSpec((None, None, page, N_HEADS, HEAD_DIM), lambda b, pt: (layer, pt[b, p], 0, 0, 0))

    heads = lambda t: t.reshape(nb, N_HEADS, HEAD_DIM)
    out = pl.pallas_call(
        functools.partial(_moba_sample_kernel, n_pages),
        grid_spec=pltpu.PrefetchScalarGridSpec(
            num_scalar_prefetch=1,
            grid=(nb,),
            in_specs=[pl.BlockSpec((N_HEADS, 1), lambda b, pt: (0, 0)), row, row, row]
                     + [page_spec(p) for p in range(n_pages)] * 2,
            out_specs=row,
            scratch_shapes=[pltpu.VMEM((N_HEADS, n_pages * page * N_HEADS), F32)],
        ),
        out_shape=jax.ShapeDtypeStruct((nb, N_HEADS, HEAD_DIM), F32),
        compiler_params=_params("arbitrary"),
        name="moba_sample",
    )(page_table, slopes.reshape(N_HEADS, 1), heads(q), heads(k_new), heads(v_new),
      *([cache_k] * n_pages), *([cache_v] * n_pages))
    return out.reshape(nb, d)


def _ffn_prompt_kernel(tiles_per_seq, x_ref, wg_ref, wu_ref, fwg_ref, fwu_ref, fbg_ref, fbu_ref, wd_ref,
                       g_ref, b_ref, y_ref, st_ref, xb_ref, acc_ref, hg_ref, hu_ref, act_ref, cg_ref, cu_ref):
    i = pl.program_id(0)
    c = pl.program_id(1)
    tm = x_ref.shape[0]
    fc = wg_ref.shape[1]
    first = (i % tiles_per_seq) == 0
    hist = FFN_CONV_W - 1

    @pl.when(c == 0)
    def _():
        xb_ref[...] = x_ref[...].astype(BF16)
        acc_ref[...] = jnp.zeros_like(acc_ref)

    for h_ref, carry_ref in ((hg_ref, cg_ref), (hu_ref, cu_ref)):
        @pl.when(first)
        def _():
            h_ref[0:SUBLANES, :] = jnp.zeros((SUBLANES, fc), F32)

        @pl.when(jnp.logical_not(first))
        def _():
            h_ref[0:SUBLANES, :] = carry_ref[c]

    def conv(h_ref, fw_ref, fb_ref, r0, cs):
        win = h_ref[r0:r0 + FFN_R + SUBLANES, cs]
        cur = slice(SUBLANES, SUBLANES + FFN_R)
        return (fw_ref[0:1, cs] * pltpu.roll(win, 2, 0)[cur, :]
                + fw_ref[1:2, cs] * pltpu.roll(win, 1, 0)[cur, :]
                + fw_ref[2:3, cs] * win[cur, :] + fb_ref[:, cs])

    sr = tm // FFN_SUB
    for sub in range(FFN_SUB):
        rs = slice(sub * sr, (sub + 1) * sr)
        hs = slice(SUBLANES + sub * sr, SUBLANES + (sub + 1) * sr)
        xs = xb_ref[rs, :]
        hg_ref[hs, :] = jnp.dot(xs, wg_ref[...], preferred_element_type=F32)
        hu_ref[hs, :] = jnp.dot(xs, wu_ref[...], preferred_element_type=F32)
        for r0 in range(sub * sr, (sub + 1) * sr, FFN_R):
            for cc in range(fc // FFN_C):
                cs = slice(cc * FFN_C, (cc + 1) * FFN_C)
                gt = conv(hg_ref, fwg_ref, fbg_ref, r0, cs)
                u = conv(hu_ref, fwu_ref, fbu_ref, r0, cs)
                act_ref[r0:r0 + FFN_R, cs] = (_silu(gt) * u).astype(BF16)
        acc = acc_ref[rs, :] + jnp.dot(act_ref[rs, :], wd_ref[...], preferred_element_type=F32)
        acc_ref[rs, :] = acc
        y_ref[rs, :] = _layer_norm(ALPHA * x_ref[rs, :] + acc, g_ref[...], b_ref[...])

    cg_ref[c] = hg_ref[tm:tm + SUBLANES, :]
    cu_ref[c] = hu_ref[tm:tm + SUBLANES, :]
    for r in range(hist):
        row = SUBLANES + tm - hist + r
        st_ref[r] = jnp.concatenate([hg_ref[row:row + 1, :], hu_ref[row:row + 1, :]], axis=0)


def _ffn_prompt(x, w_up, w_fdw, b_fdw, w_down, g, b, nb_batch, seq):
    m, d = x.shape
    f = w_down.shape[0]
    tm = TM_FFN
    _, nc, _, fc = w_up.shape
    assert seq % tm == 0 and nc * fc == f and FFN_CONV_W == 3 and tm % FFN_R == 0 and fc % FFN_C == 0
    tps = seq // tm
    vec = pl.BlockSpec((1, d), lambda i, c: (0, 0))
    y, st = pl.pallas_call(
        functools.partial(_ffn_prompt_kernel, tps),
        grid=(m // tm, nc),
        in_specs=[pl.BlockSpec((tm, d), lambda i, c: (i, 0)),
                  pl.BlockSpec((None, None, d, fc), lambda i, c: (0, c, 0, 0)),
                  pl.BlockSpec((None, None, d, fc), lambda i, c: (1, c, 0, 0)),
                  pl.BlockSpec((FFN_CONV_W, fc), lambda i, c: (0, c)),
                  pl.BlockSpec((FFN_CONV_W, fc), lambda i, c: (0, nc + c)),
                  pl.BlockSpec((1, fc), lambda i, c: (0, c)), pl.BlockSpec((1, fc), lambda i, c: (0, nc + c)),
                  pl.BlockSpec((fc, d), lambda i, c: (c, 0)), vec, vec],
        out_specs=[pl.BlockSpec((tm, d), lambda i, c: (i, 0)),
                   pl.BlockSpec((None, FFN_CONV_W - 1, 2, fc), lambda i, c: (i, 0, 0, c))],
        out_shape=[jax.ShapeDtypeStruct((m, d), F32),
                   jax.ShapeDtypeStruct((m // tm, FFN_CONV_W - 1, 2, f), F32)],
        scratch_shapes=[pltpu.VMEM((tm, d), BF16), pltpu.VMEM((tm, d), F32),
                        pltpu.VMEM((tm + SUBLANES, fc), F32), pltpu.VMEM((tm + SUBLANES, fc), F32),
                        pltpu.VMEM((tm, fc), BF16),
                        pltpu.VMEM((nc, SUBLANES, fc), F32), pltpu.VMEM((nc, SUBLANES, fc), F32)],
        compiler_params=_params("arbitrary", "arbitrary"),
        name="ffn_prompt",
    )(x, w_up, w_up, w_fdw, w_fdw, b_fdw.reshape(1, 2 * f), b_fdw.reshape(1, 2 * f), w_down,
      g.reshape(1, d), b.reshape(1, d))
    return y, st[tps - 1::tps].reshape(nb_batch, FFN_CONV_W - 1, 2 * f)


def _ffn_up_weights(w_up):
    d, f2 = w_up.shape
    f = f2 // 2
    fc = FC_FFN if f % FC_FFN == 0 else FFN_C
    return w_up.astype(BF16).reshape(d, 2, f // fc, fc).transpose(1, 2, 0, 3)


def _ffn_sample_kernel(x_ref, s0g_ref, s0u_ref, s1g_ref, s1u_ref, wg_ref, wu_ref, fwg_ref, fwu_ref,
                       fbg_ref, fbu_ref, wd_ref, g_ref, b_ref, y_ref, hg_ref, hu_ref, acc_ref):
    c = pl.program_id(0)

    @pl.when(c == 0)
    def _():
        acc_ref[...] = jnp.zeros_like(acc_ref)

    xb = x_ref[...].astype(BF16)
    hg = jnp.dot(xb, wg_ref[...], preferred_element_type=F32)
    hu = jnp.dot(xb, wu_ref[...], preferred_element_type=F32)
    hg_ref[...] = hg
    hu_ref[...] = hu
    gt = fwg_ref[0:1, :] * s0g_ref[...] + fwg_ref[1:2, :] * s1g_ref[...] + fwg_ref[2:3, :] * hg + fbg_ref[...]
    u = fwu_ref[0:1, :] * s0u_ref[...] + fwu_ref[1:2, :] * s1u_ref[...] + fwu_ref[2:3, :] * hu + fbu_ref[...]
    act = (_silu(gt) * u).astype(BF16)
    acc_ref[...] += jnp.dot(act, wd_ref[...], preferred_element_type=F32)

    @pl.when(c == pl.num_programs(0) - 1)
    def _():
        y_ref[...] = _layer_norm(ALPHA * x_ref[...] + acc_ref[...], g_ref[...], b_ref[...])


def _ffn_sample(x, state, w_up, w_fdw, b_fdw, w_down, g, b):
    nb, d = x.shape
    f = w_down.shape[0]
    _, nc, _, fc = w_up.shape
    assert nc * fc == f and FFN_CONV_W == 3 and state.shape == (nb, FFN_CONV_W - 1, 2 * f)
    st = state.reshape(nb, 4 * f)
    vec = pl.BlockSpec((1, d), lambda c: (0, 0))
    col = lambda k: pl.BlockSpec((nb, fc), lambda c: (0, k * nc + c))
    y, hg, hu = pl.pallas_call(
        _ffn_sample_kernel,
        grid=(nc,),
        in_specs=[pl.BlockSpec((nb, d), lambda c: (0, 0)), col(0), col(1), col(2), col(3),
                  pl.BlockSpec((None, None, d, fc), lambda c: (0, c, 0, 0)),
                  pl.BlockSpec((None, None, d, fc), lambda c: (1, c, 0, 0)),
                  pl.BlockSpec((FFN_CONV_W, fc), lambda c: (0, c)),
                  pl.BlockSpec((FFN_CONV_W, fc), lambda c: (0, nc + c)),
                  pl.BlockSpec((1, fc), lambda c: (0, c)), pl.BlockSpec((1, fc), lambda c: (0, nc + c)),
                  pl.BlockSpec((fc, d), lambda c: (c, 0)), vec, vec],
        out_specs=[pl.BlockSpec((nb, d), lambda c: (0, 0)), pl.BlockSpec((nb, fc), lambda c: (0, c)),
                   pl.BlockSpec((nb, fc), lambda c: (0, c))],
        out_shape=[jax.ShapeDtypeStruct((nb, d), F32), jax.ShapeDtypeStruct((nb, f), F32),
                   jax.ShapeDtypeStruct((nb, f), F32)],
        scratch_shapes=[pltpu.VMEM((nb, d), F32)],
        compiler_params=_params("arbitrary"),
        name="ffn_sample",
    )(x, st, st, st, st, w_up, w_up, w_fdw, w_fdw, b_fdw.reshape(1, 2 * f), b_fdw.reshape(1, 2 * f), w_down,
      g.reshape(1, d), b.reshape(1, d))
    new_state = jnp.stack([state[:, 1, :], jnp.concatenate([hg, hu], axis=1)], axis=1)
    return y, new_state


def kernel(x_prompt, x_sample, cache_k, cache_v, state_conv, state_ffn, page_table, w_qkv, w_o, w_pw1, b_pw1,
           w_dw, b_dw, ln_cv_g, ln_cv_b, w_pw2, b_pw2, w_up, w_fdw, b_fdw, w_down, ln1_g, ln1_b, ln2_g, ln2_b):
    bp, sp, d = x_prompt.shape
    bs, ss, _ = x_sample.shape
    assert d == D_MODEL and ss == 1
    slopes = jnp.asarray(_alibi_slopes())
    yp = x_prompt.reshape(bp * sp, d)
    ys = x_sample.reshape(bs, d)
    k_sa, v_sa, c_pr, c_sa, f_pr, f_sa = [], [], [], [], [], []
    kv_pr = None
    kv_shape = ((DEPTH + 1) // 2, bp, sp, N_HEADS, HEAD_DIM)
    for i in range(DEPTH):
        j = i // 2
        if i % 2 == 0:
            wq = w_qkv[j].astype(BF16)
            wo = w_o[j].astype(BF16)
            qa, k, v, kf, vf = _qkv_prompt(yp, wq, bp, sp, j, kv_pr, slopes)
            kv_pr = (kf, vf)
            op = _moba_prompt(qa, k, v, slopes)
            qkv_s = _mm(ys, wq)
            q_s, k_s, v_s = qkv_s[:, 0:d], qkv_s[:, d:2 * d], qkv_s[:, 2 * d:3 * d]
            os_ = _moba_sample(q_s, k_s, v_s, cache_k, cache_v, page_table, j, slopes)
            yp = _proj_ln(op, wo, None, yp, ln1_g[i], ln1_b[i])
            ys = _proj_ln(os_, wo, None, ys, ln1_g[i], ln1_b[i])
            k_sa.append(k_s.reshape(bs, ss, N_HEADS, HEAD_DIM))
            v_sa.append(v_s.reshape(bs, ss, N_HEADS, HEAD_DIM))
        else:
            w1 = w_pw1[j].astype(BF16)
            w2 = w_pw2[j].astype(BF16)
            up = _pw1_glu(yp, w1, b_pw1[j])
            zp = _dwconv_ln_prompt(up, w_dw[j], b_dw[j], ln_cv_g[j], ln_cv_b[j], sp)
            us = _pw1_glu(ys, w1, b_pw1[j])
            zs = _dwconv_ln_sample(state_conv[j], us, w_dw[j], b_dw[j], ln_cv_g[j], ln_cv_b[j])
            yp = _proj_ln(zp, w2, b_pw2[j], yp, ln1_g[i], ln1_b[i])
            ys = _proj_ln(zs, w2, b_pw2[j], ys, ln1_g[i], ln1_b[i])
            c_pr.append(up.reshape(bp, sp, d)[:, sp - (CONV_W - 1):, :])
            c_sa.append(jnp.concatenate([state_conv[j][:, 1:, :], us[:, None, :]], axis=1))
        wu = _ffn_up_weights(w_up[i])
        wd = w_down[i].astype(BF16)
        yp, fp = _ffn_prompt(yp, wu, w_fdw[i], b_fdw[i], wd, ln2_g[i], ln2_b[i], bp, sp)
        ys, fs = _ffn_sample(ys, state_ffn[i], wu, w_fdw[i], b_fdw[i], wd, ln2_g[i], ln2_b[i])
        f_pr.append(fp)
        f_sa.append(fs)
    return (yp.reshape(bp, sp, d), ys.reshape(bs, ss, d), kv_pr[0].reshape(kv_shape), kv_pr[1].reshape(kv_shape),
            jnp.stack(k_sa),
            jnp.stack(v_sa), jnp.stack(c_pr), jnp.stack(c_sa), jnp.stack(f_pr), jnp.stack(f_sa))
```

```python
import functools
import math

import numpy as np
import jax
import jax.numpy as jnp
from jax import lax
from jax.experimental import pallas as pl
from jax.experimental.pallas import tpu as pltpu

N_HEADS = 8
HEAD_DIM = 128
D_MODEL = N_HEADS * HEAD_DIM
BLOCK = 256
TOPK = 3
CONV_W = 31
FFN_CONV_W = 3
DEPTH = 4
ALPHA = (2.0 * DEPTH) ** 0.25
LN_EPS = 1e-5
NEG = -1e30
SCALE = 1.0 / math.sqrt(HEAD_DIM)

LANES = 128
SUBLANES = 8
VMEM_LIMIT = 56 * 1024 * 1024

F32 = jnp.float32
BF16 = jnp.bfloat16

TM_ROWS = 512
TM_CONV = 256
CONV_HALO = 32
CONV_R = 64
CONV_C = 128
MOBA_QT = 1024
MOBA_KT = 1024
TM_FFN = 512
FC_FFN = 1408
FFN_SUB = 2
FFN_R = 32
FFN_C = 128
FC_FFN_S = 256
TB_CONV_S = 32

_NT = (((1,), (1,)), ((), ()))


def _alibi_slopes():
    s = np.exp2(-8.0 * (np.arange(N_HEADS, dtype=np.float64) + 1.0) / N_HEADS).astype(np.float32)
    assert np.all(s.astype(jnp.bfloat16).astype(np.float32) == s)
    assert np.all(np.log2(s) == np.round(np.log2(s)))
    return s


def _params(*sem):
    return pltpu.CompilerParams(dimension_semantics=sem, vmem_limit_bytes=VMEM_LIMIT)


def _layer_norm(x, g, b):
    mu = jnp.mean(x, axis=-1, keepdims=True)
    xc = x - mu
    var = jnp.mean(xc * xc, axis=-1, keepdims=True)
    return xc * lax.rsqrt(var + LN_EPS) * g + b


def _silu(x):
    return x * jax.nn.sigmoid(x)


def _qkv_prompt_kernel(n_aliased, slopes_ref, x_ref, w_ref, *refs):
    qa_ref, k_ref, v_ref, kf_ref, vf_ref, kmt_ref = refs[n_aliased:]
    i = pl.program_id(1)
    tm = x_ref.shape[0]
    nb = kmt_ref.shape[0]
    bpt = tm // BLOCK
    xb = x_ref[...].astype(BF16)
    k = jnp.dot(xb, w_ref[:, D_MODEL:2 * D_MODEL], preferred_element_type=F32)
    kf_ref[...] = pltpu.einshape("m(hd)->mhd", k, h=N_HEADS)
    for h in range(N_HEADS):
        k_ref[h] = k[:, h * HEAD_DIM:(h + 1) * HEAD_DIM].astype(BF16)

    @pl.when(i == 0)
    def _():
        kmt_ref[...] = jnp.zeros_like(kmt_ref)

    kmt = kmt_ref[...]
    blk = lax.broadcasted_iota(jnp.int32, kmt.shape, 0)
    for j in range(bpt):
        kmt = jnp.where(blk == bpt * i + j, jnp.mean(k[j * BLOCK:(j + 1) * BLOCK], axis=0, keepdims=True), kmt)
    kmt_ref[...] = kmt

    q = jnp.dot(xb, w_ref[:, 0:D_MODEL], preferred_element_type=F32)
    qs = q * SCALE
    eye = jnp.where(lax.broadcasted_iota(jnp.int32, (BLOCK, BLOCK), 0)
                    == lax.broadcasted_iota(jnp.int32, (BLOCK, BLOCK), 1), 1.0, 0.0).astype(BF16)
    for h in range(N_HEADS):
        sl = slice(h * HEAD_DIM, (h + 1) * HEAD_DIM)
        qa_ref[h] = _query_operand(q[:, sl], qs[:, sl], kmt[:, sl], slopes_ref[h], bpt * i, nb, eye)
    v = jnp.dot(xb, w_ref[:, 2 * D_MODEL:3 * D_MODEL], preferred_element_type=F32)
    vf_ref[...] = pltpu.einshape("m(hd)->mhd", v, h=N_HEADS)
    ones = jnp.ones((tm, HEAD_DIM), BF16)
    for h in range(N_HEADS):
        v_ref[h] = jnp.concatenate([v[:, h * HEAD_DIM:(h + 1) * HEAD_DIM].astype(BF16), ones], axis=1)


def _qkv_prompt(x, w, nb_batch, seq, layer, kv_prev, slopes):
    m, d = x.shape
    tm = TM_ROWS
    assert seq % tm == 0 and tm % BLOCK == 0 and d == D_MODEL and seq // BLOCK + 4 <= HEAD_DIM
    nt = seq // tm
    n_attn = (DEPTH + 1) // 2
    hm = jax.ShapeDtypeStruct((nb_batch, N_HEADS, seq, HEAD_DIM), BF16)
    hm_spec = pl.BlockSpec((None, N_HEADS, tm, HEAD_DIM), lambda b, i: (b, 0, i, 0))
    hm2 = jax.ShapeDtypeStruct((nb_batch, N_HEADS, seq, 2 * HEAD_DIM), BF16)
    hm2_spec = pl.BlockSpec((None, N_HEADS, tm, 2 * HEAD_DIM), lambda b, i: (b, 0, i, 0))
    row_spec = pl.BlockSpec((tm, d), lambda b, i: (b * nt + i, 0))
    kv = jax.ShapeDtypeStruct((n_attn, m, N_HEADS, HEAD_DIM), F32)
    kv_spec = pl.BlockSpec((None, tm, N_HEADS, HEAD_DIM), lambda b, i: (layer, b * nt + i, 0, 0))
    in_specs = [pl.BlockSpec(memory_space=pltpu.SMEM), row_spec, pl.BlockSpec((d, 3 * d), lambda b, i: (0, 0))]
    args = [slopes, x, w]
    aliases = {}
    if kv_prev is not None:
        in_specs += [pl.BlockSpec(memory_space=pl.ANY)] * 2
        args += list(kv_prev)
        aliases = {3: 3, 4: 4}
    return pl.pallas_call(
        functools.partial(_qkv_prompt_kernel, len(aliases)),
        grid=(nb_batch, nt),
        in_specs=in_specs,
        out_specs=[hm2_spec, hm_spec, hm2_spec, kv_spec, kv_spec],
        out_shape=[hm2, hm, hm2, kv, kv],
        scratch_shapes=[pltpu.VMEM((seq // BLOCK, d), F32)],
        input_output_aliases=aliases,
        compiler_params=_params("parallel", "arbitrary"),
        name="qkv_prompt",
    )(*args)


def _mm_kernel(x_ref, w_ref, o_ref):
    o_ref[...] = jnp.dot(x_ref[...].astype(BF16), w_ref[...], preferred_element_type=F32)


def _mm(x, w, tn=1024):
    m, k = x.shape
    n = w.shape[1]
    assert n % tn == 0
    return pl.pallas_call(
        _mm_kernel,
        grid=(n // tn,),
        in_specs=[pl.BlockSpec((m, k), lambda j: (0, 0)), pl.BlockSpec((k, tn), lambda j: (0, j))],
        out_specs=pl.BlockSpec((m, tn), lambda j: (0, j)),
        out_shape=jax.ShapeDtypeStruct((m, n), F32),
        compiler_params=_params("parallel"),
        name="mm_rows",
    )(x, w)


def _proj_ln_kernel(has_bias, a_ref, w_ref, *refs):
    if has_bias:
        bias_ref, y_ref, g_ref, b_ref, o_ref = refs
    else:
        y_ref, g_ref, b_ref, o_ref = refs
    mp = jnp.dot(a_ref[...].astype(BF16), w_ref[...], preferred_element_type=F32)
    if has_bias:
        mp = mp + bias_ref[...]
    o_ref[...] = _layer_norm(ALPHA * y_ref[...] + mp, g_ref[...], b_ref[...])


def _proj_ln(a, w, bias, y, g, b):
    m, d = y.shape
    tm = min(TM_ROWS, m)
    assert m % tm == 0
    row = lambda i: (i, 0)
    fixed = lambda i: (0, 0)
    vec = pl.BlockSpec((1, d), fixed)
    in_specs = [pl.BlockSpec((tm, a.shape[1]), row), pl.BlockSpec(w.shape, fixed)]
    args = [a, w]
    if bias is not None:
        in_specs.append(vec)
        args.append(bias.reshape(1, d))
    in_specs += [pl.BlockSpec((tm, d), row), vec, vec]
    args += [y, g.reshape(1, d), b.reshape(1, d)]
    return pl.pallas_call(
        functools.partial(_proj_ln_kernel, bias is not None),
        grid=(m // tm,),
        in_specs=in_specs,
        out_specs=pl.BlockSpec((tm, d), row),
        out_shape=jax.ShapeDtypeStruct((m, d), F32),
        compiler_params=_params("parallel"),
        name="proj_ln",
    )(*args)


def _pw1_glu_kernel(x_ref, w_ref, b_ref, u_ref):
    d = u_ref.shape[1]
    xb = x_ref[...].astype(BF16)
    a = jnp.dot(xb, w_ref[:, 0:d], preferred_element_type=F32) + b_ref[:, 0:d]
    gt = jnp.dot(xb, w_ref[:, d:2 * d], preferred_element_type=F32) + b_ref[:, d:2 * d]
    u_ref[...] = a * jax.nn.sigmoid(gt)


def _pw1_glu(x, w, bias):
    m, d = x.shape
    tm = min(TM_ROWS, m)
    assert m % tm == 0
    return pl.pallas_call(
        _pw1_glu_kernel,
        grid=(m // tm,),
        in_specs=[pl.BlockSpec((tm, d), lambda i: (i, 0)), pl.BlockSpec((d, 2 * d), lambda i: (0, 0)),
                  pl.BlockSpec((1, 2 * d), lambda i: (0, 0))],
        out_specs=pl.BlockSpec((tm, d), lambda i: (i, 0)),
        out_shape=jax.ShapeDtypeStruct((m, d), F32),
        compiler_params=_params("parallel"),
        name="pw1_glu",
    )(x, w, bias.reshape(1, 2 * d))


def _dwconv_ln_prompt_kernel(tiles_per_seq, u_ref, halo_ref, w_ref, b_ref, g_ref, be_ref, o_ref, ext_ref, y_ref):
    i = pl.program_id(0)
    tm, d = u_ref.shape
    first = (i % tiles_per_seq) == 0
    ext_ref[0:CONV_HALO, :] = jnp.where(first, 0.0, halo_ref[...])
    ext_ref[CONV_HALO:CONV_HALO + tm, :] = u_ref[...]
    off = CONV_HALO - (CONV_W - 1)

    def row_body(r, carry):
        r0 = pl.multiple_of(r * CONV_R, CONV_R)
        for c in range(d // CONV_C):
            cs = slice(c * CONV_C, (c + 1) * CONV_C)
            win = ext_ref[pl.ds(r0, CONV_R + CONV_HALO), cs]
            acc = jnp.zeros((CONV_R, CONV_C), F32)
            for ph in range(SUBLANES):
                taps = [k for k in range(CONV_W) if (off + k) % SUBLANES == ph]
                if not taps:
                    continue
                wp = win if ph == 0 else pltpu.roll(win, CONV_R + CONV_HALO - ph, 0)
                for k in taps:
                    a = off + k - ph
                    acc = acc + wp[a:a + CONV_R, :] * w_ref[k:k + 1, cs]
            y_ref[pl.ds(r0, CONV_R), cs] = acc + b_ref[:, cs]
        return carry

    lax.fori_loop(0, tm // CONV_R, row_body, 0)
    z = _layer_norm(y_ref[...], g_ref[...], be_ref[...])
    o_ref[...] = _silu(z).astype(BF16)


def _dwconv_ln_prompt(u, w, bias, g, b, seq):
    m, d = u.shape
    tm = TM_CONV
    assert seq % tm == 0 and tm % CONV_R == 0 and tm % CONV_HALO == 0 and d % CONV_C == 0
    assert CONV_HALO >= CONV_W - 1 and CONV_HALO % SUBLANES == 0
    hb = tm // CONV_HALO
    vec = pl.BlockSpec((1, d), lambda i: (0, 0))
    return pl.pallas_call(
        functools.partial(_dwconv_ln_prompt_kernel, seq // tm),
        grid=(m // tm,),
        in_specs=[pl.BlockSpec((tm, d), lambda i: (i, 0)),
                  pl.BlockSpec((CONV_HALO, d), lambda i: (jnp.maximum(i * hb - 1, 0), 0)),
                  pl.BlockSpec((CONV_W, d), lambda i: (0, 0)), vec, vec, vec],
        out_specs=pl.BlockSpec((tm, d), lambda i: (i, 0)),
        out_shape=jax.ShapeDtypeStruct((m, d), BF16),
        scratch_shapes=[pltpu.VMEM((tm + CONV_HALO, d), F32), pltpu.VMEM((tm, d), F32)],
        compiler_params=_params("parallel"),
        name="dwconv_ln_prompt",
    )(u, u, w, bias.reshape(1, d), g.reshape(1, d), b.reshape(1, d))


def _dwconv_ln_sample_kernel(st_ref, u_ref, w_ref, b_ref, g_ref, be_ref, o_ref):
    st = st_ref[...]
    w = w_ref[...]
    y = jnp.sum(st * w[None, 0:CONV_W - 1, :], axis=1)
    y = y + u_ref[...] * w[CONV_W - 1:CONV_W, :] + b_ref[...]
    z = _layer_norm(y, g_ref[...], be_ref[...])
    o_ref[...] = _silu(z).astype(BF16)


def _dwconv_ln_sample(state, u, w, bias, g, b):
    nb, hist, d = state.shape
    tb = min(TB_CONV_S, nb)
    assert nb % tb == 0 and hist == CONV_W - 1
    vec = pl.BlockSpec((1, d), lambda i: (0, 0))
    return pl.pallas_call(
        _dwconv_ln_sample_kernel,
        grid=(nb // tb,),
        in_specs=[pl.BlockSpec((tb, hist, d), lambda i: (i, 0, 0)), pl.BlockSpec((tb, d), lambda i: (i, 0)),
                  pl.BlockSpec((CONV_W, d), lambda i: (0, 0)), vec, vec, vec],
        out_specs=pl.BlockSpec((tb, d), lambda i: (i, 0)),
        out_shape=jax.ShapeDtypeStruct((nb, d), BF16),
        compiler_params=_params("parallel"),
        name="dwconv_ln_sample",
    )(state, u, w, bias.reshape(1, d), g.reshape(1, d), b.reshape(1, d))


def _aug_lanes(nb):
    return dict(tq=nb, r=nb + 1, kb=nb + 2, qb=nb + 3)


def _query_operand(q, qs, km, slope, first_block, nb, eye):
    rows = q.shape[0]
    ln = _aug_lanes(nb)
    gate = lax.dot_general(km.astype(BF16), q.astype(BF16), _NT, preferred_element_type=F32)
    blk = lax.broadcasted_iota(jnp.int32, (nb, rows), 0)
    own_t = first_block + lax.broadcasted_iota(jnp.int32, (nb, rows), 1) // BLOCK
    past = blk < own_t
    gw = jnp.where(past, gate, -jnp.inf)
    valid_bias = jnp.where(past, 0.0, NEG)
    sel_t = jnp.full((nb, rows), NEG, F32)
    blkf = blk.astype(F32)
    for _ in range(TOPK):
        mx = jnp.max(gw, axis=0, keepdims=True)
        idx = jnp.min(jnp.where(gw == mx, blkf, float(nb)), axis=0, keepdims=True)
        pick = blkf == idx
        sel_t = jnp.where(pick, valid_bias, sel_t)
        gw = jnp.where(pick, -jnp.inf, gw)
    sel_t = jnp.where(blk == own_t, 0.0, sel_t)
    sel_p = jnp.concatenate([sel_t.astype(BF16), jnp.zeros((HEAD_DIM - nb, rows), BF16)], axis=0)
    selbias = jnp.concatenate(
        [lax.dot_general(eye, sel_p[:, a * BLOCK:(a + 1) * BLOCK], _NT, preferred_element_type=F32)
         for a in range(rows // BLOCK)], axis=0)

    lane = lax.broadcasted_iota(jnp.int32, (rows, HEAD_DIM), 1)
    rowi = lax.broadcasted_iota(jnp.int32, (rows, HEAD_DIM), 0)
    own = first_block + rowi // BLOCK
    aug = jnp.where(lane < nb, selbias, 0.0)
    aug = jnp.where(lane == ln["tq"], -slope * (rowi % BLOCK).astype(F32), aug)
    aug = jnp.where((lane == ln["r"]) | (lane == ln["kb"]), 1.0, aug)
    aug = jnp.where(lane == ln["qb"], -slope * (BLOCK * own).astype(F32), aug)
    return jnp.concatenate([qs.astype(BF16), aug.astype(BF16)], axis=1)


def _moba_prompt_kernel(slopes_ref, qa_ref, k_ref, v_ref, o_ref, tab_ref, acc_ref):
    h = pl.program_id(1)
    i = pl.program_id(2)
    qt = qa_ref.shape[0]
    nsub = qt // BLOCK
    nb = k_ref.shape[0] // BLOCK
    ln = _aug_lanes(nb)
    slope = slopes_ref[h]

    @pl.when(i == 0)
    def _build_key_table():
        lane_b = lax.broadcasted_iota(jnp.int32, (BLOCK, HEAD_DIM), 1)
        rowf_b = lax.broadcasted_iota(jnp.int32, (BLOCK, HEAD_DIM), 0).astype(F32)
        base = jnp.where((lane_b == ln["tq"]) | (lane_b == ln["qb"]), 1.0, 0.0)
        base = jnp.where(lane_b == ln["r"], slope * rowf_b, base)

        def tab_body(b, carry):
            t = jnp.where(lane_b == b, 1.0, base)
            t = jnp.where(lane_b == ln["kb"], slope * jnp.asarray(BLOCK * b, F32), t)
            tab_ref[pl.ds(pl.multiple_of(b * BLOCK, BLOCK), BLOCK), :] = t.astype(BF16)
            return carry

        lax.fori_loop(0, nb, tab_body, 0)

    qas = [qa_ref[a * BLOCK:(a + 1) * BLOCK, :] for a in range(nsub)]

    def key_tile(j):
        rows = pl.ds(pl.multiple_of(j * MOBA_KT, MOBA_KT), MOBA_KT)
        return jnp.concatenate([k_ref[rows, :], tab_ref[rows, :]], axis=1), v_ref[rows, :]

    ka, vb = key_tile(i)
    ms = []
    for a in range(nsub):
        nk = (a + 1) * BLOCK
        s = lax.dot_general(qas[a], ka[0:nk, :], _NT, preferred_element_type=F32)
        diff = (lax.broadcasted_iota(jnp.int32, (BLOCK, nk), 1)
                - lax.broadcasted_iota(jnp.int32, (BLOCK, nk), 0))
        s = jnp.where(diff <= a * BLOCK, s, NEG)
        m0 = jnp.max(s, axis=1, keepdims=True)
        ms.append(m0)
        acc_ref[a] = jnp.dot(jnp.exp(s - m0).astype(BF16), vb[0:nk, :], preferred_element_type=F32)

    def tile_body(j, ms):
        ka, vb = key_tile(j)
        out = []
        for a in range(nsub):
            s = lax.dot_general(qas[a], ka, _NT, preferred_element_type=F32)
            m_new = jnp.maximum(ms[a], jnp.max(s, axis=1, keepdims=True))
            out.append(m_new)
            acc_ref[a] = (jnp.exp(ms[a] - m_new) * acc_ref[a]
                          + jnp.dot(jnp.exp(s - m_new).astype(BF16), vb, preferred_element_type=F32))
        return tuple(out)

    def pair_body(jj, ms):
        return tile_body(2 * jj + 1, tile_body(2 * jj, ms))

    ms = lax.fori_loop(0, i // 2, pair_body, tuple(ms))
    lax.fori_loop(2 * (i // 2), i, tile_body, ms)
    for a in range(nsub):
        acc = acc_ref[a]
        o_ref[a * BLOCK:(a + 1) * BLOCK, :] = (acc[:, 0:HEAD_DIM] / acc[:, HEAD_DIM:2 * HEAD_DIM]).astype(BF16)


def _moba_prompt(qa, k, v, slopes):
    nbt, nh, seq, hd = k.shape
    qt = MOBA_QT
    assert seq % qt == 0 and MOBA_KT == qt and qt % BLOCK == 0
    assert hd == HEAD_DIM == LANES and nh == N_HEADS and v.shape[-1] == 2 * hd and qa.shape[-1] == 2 * hd
    nq = seq // qt
    assert seq // BLOCK + 4 <= HEAD_DIM
    qspec = pl.BlockSpec((None, None, qt, 2 * hd), lambda b, h, i: (b, h, i, 0))
    kspec = pl.BlockSpec((None, None, seq, hd), lambda b, h, i: (b, h, 0, 0))
    vspec = pl.BlockSpec((None, None, seq, 2 * hd), lambda b, h, i: (b, h, 0, 0))
    return pl.pallas_call(
        _moba_prompt_kernel,
        grid=(nbt, nh, nq),
        in_specs=[pl.BlockSpec(memory_space=pltpu.SMEM), qspec, kspec, vspec],
        out_specs=pl.BlockSpec((qt, hd), lambda b, h, i: (b * nq + i, h)),
        out_shape=jax.ShapeDtypeStruct((nbt * seq, nh * hd), BF16),
        scratch_shapes=[pltpu.VMEM((seq, hd), BF16), pltpu.VMEM((qt // BLOCK, BLOCK, 2 * hd), F32)],
        compiler_params=_params("parallel", "parallel", "arbitrary"),
        name="moba_prompt",
    )(slopes, qa, k, v)


def _moba_sample_kernel(n_pages, pt_ref, slopes_ref, q_ref, kn_ref, vn_ref, *refs):
    del pt_ref
    k_refs = refs[:n_pages]
    v_refs = refs[n_pages:2 * n_pages]
    o_ref = refs[2 * n_pages]
    base_ref = refs[2 * n_pages + 1]
    page = k_refs[0].shape[0]
    past_len = n_pages * page
    nbp = past_len // BLOCK
    ppb = BLOCK // page
    lpp = page * N_HEADS
    lpb = BLOCK * N_HEADS
    lane_b = lax.broadcasted_iota(jnp.int32, (N_HEADS, lpb), 1)
    row_b = lax.broadcasted_iota(jnp.int32, (N_HEADS, lpb), 0)
    own_head = (lane_b % N_HEADS) == row_b

    @pl.when(pl.program_id(0) == 0)
    def _build_bias():
        slopes = slopes_ref[...]
        for b in range(nbp):
            dist = (past_len - b * BLOCK - lane_b // N_HEADS).astype(F32)
            base_ref[:, b * lpb:(b + 1) * lpb] = jnp.where(own_head, -slopes * dist, NEG)

    q8 = q_ref[...]
    qb = q8.astype(BF16)

    raws = []
    for b in range(nbp):
        parts = []
        for pg in range(b * ppb, (b + 1) * ppb):
            kf = k_refs[pg][...].reshape(lpp, HEAD_DIM).astype(BF16)
            parts.append(lax.dot_general(qb, kf, _NT, preferred_element_type=F32))
        raws.append(jnp.concatenate(parts, axis=1))

    lane = lax.broadcasted_iota(jnp.int32, (N_HEADS, LANES), 1)
    gw = jnp.full((N_HEADS, LANES), -jnp.inf, F32)
    for b in range(nbp):
        gb = jnp.sum(jnp.where(own_head, raws[b], 0.0), axis=1, keepdims=True) * (1.0 / BLOCK)
        gw = jnp.where(lane == b, gb, gw)
    sel = jnp.zeros((N_HEADS, LANES), F32)
    lanef = lane.astype(F32)
    for _ in range(min(TOPK, nbp)):
        mx = jnp.max(gw, axis=1, keepdims=True)
        idx = jnp.min(jnp.where(gw == mx, lanef, float(LANES)), axis=1, keepdims=True)
        pick = lanef == idx
        sel = jnp.where(pick, 1.0, sel)
        gw = jnp.where(pick, -jnp.inf, gw)

    scores = [jnp.where(sel[:, b:b + 1] > 0.5, raws[b] * SCALE + base_ref[:, b * lpb:(b + 1) * lpb], NEG)
              for b in range(nbp)]
    s_new = jnp.sum(q8 * kn_ref[...], axis=1, keepdims=True) * SCALE

    m = s_new
    for b in range(nbp):
        m = jnp.maximum(m, jnp.max(scores[b], axis=1, keepdims=True))
    p_new = jnp.exp(s_new - m)
    l = p_new
    acc = p_new * vn_ref[...]
    for b in range(nbp):
        p = jnp.exp(scores[b] - m)
        l = l + jnp.sum(p, axis=1, keepdims=True)
        pb = p.astype(BF16)
        for t in range(ppb):
            vf = v_refs[b * ppb + t][...].reshape(lpp, HEAD_DIM).astype(BF16)
            acc = acc + jnp.dot(pb[:, t * lpp:(t + 1) * lpp], vf, preferred_element_type=F32)
    o_ref[...] = acc / l


def _moba_sample(q, k_new, v_new, cache_k, cache_v, page_table, layer, slopes):
    nb, d = q.shape
    n_pages = page_table.shape[1]
    page = cache_k.shape[2]
    assert BLOCK % page == 0 and (n_pages * page) % BLOCK == 0 and d == D_MODEL
    assert cache_k.shape[3:] == (N_HEADS, HEAD_DIM) and N_HEADS == SUBLANES
    row = pl.BlockSpec((None, N_HEADS, HEAD_DIM), lambda b, pt: (b, 0, 0))

    def page_spec(p):
        return pl.BlockSpec((None, None, page, N_HEADS, HEAD_DIM), lambda b, pt: (layer, pt[b, p], 0, 0, 0))

    heads = lambda t: t.reshape(nb, N_HEADS, HEAD_DIM)
    out = pl.pallas_call(
        functools.partial(_moba_sample_kernel, n_pages),
        grid_spec=pltpu.PrefetchScalarGridSpec(
            num_scalar_prefetch=1,
            grid=(nb,),
            in_specs=[pl.BlockSpec((N_HEADS, 1), lambda b, pt: (0, 0)), row, row, row]
                     + [page_spec(p) for p in range(n_pages)] * 2,
            out_specs=row,
            scratch_shapes=[pltpu.VMEM((N_HEADS, n_pages * page * N_HEADS), F32)],
        ),
        out_shape=jax.ShapeDtypeStruct((nb, N_HEADS, HEAD_DIM), F32),
        compiler_params=_params("arbitrary"),
        name="moba_sample",
    )(page_table, slopes.reshape(N_HEADS, 1), heads(q), heads(k_new), heads(v_new),
      *([cache_k] * n_pages), *([cache_v] * n_pages))
    return out.reshape(nb, d)


def _ffn_prompt_kernel(tiles_per_seq, x_ref, wg_ref, wu_ref, fwg_ref, fwu_ref, fbg_ref, fbu_ref, wd_ref,
                       g_ref, b_ref, y_ref, st_ref, xb_ref, acc_ref, hg_ref, hu_ref, act_ref, cg_ref, cu_ref):
    i = pl.program_id(0)
    c = pl.program_id(1)
    tm = x_ref.shape[0]
    fc = wg_ref.shape[1]
    first = (i % tiles_per_seq) == 0
    hist = FFN_CONV_W - 1

    @pl.when(c == 0)
    def _():
        xb_ref[...] = x_ref[...].astype(BF16)
        acc_ref[...] = jnp.zeros_like(acc_ref)

    for h_ref, carry_ref in ((hg_ref, cg_ref), (hu_ref, cu_ref)):
        @pl.when(first)
        def _():
            h_ref[0:SUBLANES, :] = jnp.zeros((SUBLANES, fc), F32)

        @pl.when(jnp.logical_not(first))
        def _():
            h_ref[0:SUBLANES, :] = carry_ref[c]

    def conv(h_ref, fw_ref, fb_ref, r0, cs):
        win = h_ref[r0:r0 + FFN_R + SUBLANES, cs]
        cur = slice(SUBLANES, SUBLANES + FFN_R)
        return (fw_ref[0:1, cs] * pltpu.roll(win, 2, 0)[cur, :]
                + fw_ref[1:2, cs] * pltpu.roll(win, 1, 0)[cur, :]
                + fw_ref[2:3, cs] * win[cur, :] + fb_ref[:, cs])

    sr = tm // FFN_SUB
    for sub in range(FFN_SUB):
        rs = slice(sub * sr, (sub + 1) * sr)
        hs = slice(SUBLANES + sub * sr, SUBLANES + (sub + 1) * sr)
        xs = xb_ref[rs, :]
        hg_ref[hs, :] = jnp.dot(xs, wg_ref[...], preferred_element_type=F32)
        hu_ref[hs, :] = jnp.dot(xs, wu_ref[...], preferred_element_type=F32)
        for r0 in range(sub * sr, (sub + 1) * sr, FFN_R):
            for cc in range(fc // FFN_C):
                cs = slice(cc * FFN_C, (cc + 1) * FFN_C)
                gt = conv(hg_ref, fwg_ref, fbg_ref, r0, cs)
                u = conv(hu_ref, fwu_ref, fbu_ref, r0, cs)
                act_ref[r0:r0 + FFN_R, cs] = (_silu(gt) * u).astype(BF16)
        acc = acc_ref[rs, :] + jnp.dot(act_ref[rs, :], wd_ref[...], preferred_element_type=F32)
        acc_ref[rs, :] = acc
        y_ref[rs, :] = _layer_norm(ALPHA * x_ref[rs, :] + acc, g_ref[...], b_ref[...])

    cg_ref[c] = hg_ref[tm:tm + SUBLANES, :]
    cu_ref[c] = hu_ref[tm:tm + SUBLANES, :]
    for r in range(hist):
        row = SUBLANES + tm - hist + r
        st_ref[r] = jnp.concatenate([hg_ref[row:row + 1, :], hu_ref[row:row + 1, :]], axis=0)


def _ffn_prompt(x, w_up, w_fdw, b_fdw, w_down, g, b, nb_batch, seq):
    m, d = x.shape
    f = w_down.shape[0]
    tm = TM_FFN
    fc = FC_FFN if f % FC_FFN == 0 else FFN_C
    assert seq % tm == 0 and f % fc == 0 and FFN_CONV_W == 3 and tm % FFN_R == 0 and fc % FFN_C == 0
    nc = f // fc
    tps = seq // tm
    vec = pl.BlockSpec((1, d), lambda i, c: (0, 0))
    y, st = pl.pallas_call(
        functools.partial(_ffn_prompt_kernel, tps),
        grid=(m // tm, nc),
        in_specs=[pl.BlockSpec((tm, d), lambda i, c: (i, 0)),
                  pl.BlockSpec((d, fc), lambda i, c: (0, c)), pl.BlockSpec((d, fc), lambda i, c: (0, nc + c)),
                  pl.BlockSpec((FFN_CONV_W, fc), lambda i, c: (0, c)),
                  pl.BlockSpec((FFN_CONV_W, fc), lambda i, c: (0, nc + c)),
                  pl.BlockSpec((1, fc), lambda i, c: (0, c)), pl.BlockSpec((1, fc), lambda i, c: (0, nc + c)),
                  pl.BlockSpec((fc, d), lambda i, c: (c, 0)), vec, vec],
        out_specs=[pl.BlockSpec((tm, d), lambda i, c: (i, 0)),
                   pl.BlockSpec((None, FFN_CONV_W - 1, 2, fc), lambda i, c: (i, 0, 0, c))],
        out_shape=[jax.ShapeDtypeStruct((m, d), F32),
                   jax.ShapeDtypeStruct((m // tm, FFN_CONV_W - 1, 2, f), F32)],
        scratch_shapes=[pltpu.VMEM((tm, d), BF16), pltpu.VMEM((tm, d), F32),
                        pltpu.VMEM((tm + SUBLANES, fc), F32), pltpu.VMEM((tm + SUBLANES, fc), F32),
                        pltpu.VMEM((tm, fc), BF16),
                        pltpu.VMEM((nc, SUBLANES, fc), F32), pltpu.VMEM((nc, SUBLANES, fc), F32)],
        compiler_params=_params("arbitrary", "arbitrary"),
        name="ffn_prompt",
    )(x, w_up, w_up, w_fdw, w_fdw, b_fdw.reshape(1, 2 * f), b_fdw.reshape(1, 2 * f), w_down,
      g.reshape(1, d), b.reshape(1, d))
    return y, st[tps - 1::tps].reshape(nb_batch, FFN_CONV_W - 1, 2 * f)


def _ffn_sample_kernel(x_ref, s0g_ref, s0u_ref, s1g_ref, s1u_ref, wg_ref, wu_ref, fwg_ref, fwu_ref,
                       fbg_ref, fbu_ref, wd_ref, g_ref, b_ref, y_ref, hg_ref, hu_ref, acc_ref):
    c = pl.program_id(0)

    @pl.when(c == 0)
    def _():
        acc_ref[...] = jnp.zeros_like(acc_ref)

    xb = x_ref[...].astype(BF16)
    hg = jnp.dot(xb, wg_ref[...], preferred_element_type=F32)
    hu = jnp.dot(xb, wu_ref[...], preferred_element_type=F32)
    hg_ref[...] = hg
    hu_ref[...] = hu
    gt = fwg_ref[0:1, :] * s0g_ref[...] + fwg_ref[1:2, :] * s1g_ref[...] + fwg_ref[2:3, :] * hg + fbg_ref[...]
    u = fwu_ref[0:1, :] * s0u_ref[...] + fwu_ref[1:2, :] * s1u_ref[...] + fwu_ref[2:3, :] * hu + fbu_ref[...]
    act = (_silu(gt) * u).astype(BF16)
    acc_ref[...] += jnp.dot(act, wd_ref[...], preferred_element_type=F32)

    @pl.when(c == pl.num_programs(0) - 1)
    def _():
        y_ref[...] = _layer_norm(ALPHA * x_ref[...] + acc_ref[...], g_ref[...], b_ref[...])


def _ffn_sample(x, state, w_up, w_fdw, b_fdw, w_down, g, b):
    nb, d = x.shape
    f = w_down.shape[0]
    fc = FC_FFN_S
    assert f % fc == 0 and FFN_CONV_W == 3 and state.shape == (nb, FFN_CONV_W - 1, 2 * f)
    nc = f // fc
    st = state.reshape(nb, 4 * f)
    vec = pl.BlockSpec((1, d), lambda c: (0, 0))
    col = lambda k: pl.BlockSpec((nb, fc), lambda c: (0, k * nc + c))
    y, hg, hu = pl.pallas_call(
        _ffn_sample_kernel,
        grid=(nc,),
        in_specs=[pl.BlockSpec((nb, d), lambda c: (0, 0)), col(0), col(1), col(2), col(3),
                  pl.BlockSpec((d, fc), lambda c: (0, c)), pl.BlockSpec((d, fc), lambda c: (0, nc + c)),
                  pl.BlockSpec((FFN_CONV_W, fc), lambda c: (0, c)),
                  pl.BlockSpec((FFN_CONV_W, fc), lambda c: (0, nc + c)),
                  pl.BlockSpec((1, fc), lambda c: (0, c)), pl.BlockSpec((1, fc), lambda c: (0, nc + c)),
                  pl.BlockSpec((fc, d), lambda c: (c, 0)), vec, vec],
        out_specs=[pl.BlockSpec((nb, d), lambda c: (0, 0)), pl.BlockSpec((nb, fc), lambda c: (0, c)),
                   pl.BlockSpec((nb, fc), lambda c: (0, c))],
        out_shape=[jax.ShapeDtypeStruct((nb, d), F32), jax.ShapeDtypeStruct((nb, f), F32),
                   jax.ShapeDtypeStruct((nb, f), F32)],
        scratch_shapes=[pltpu.VMEM((nb, d), F32)],
        compiler_params=_params("arbitrary"),
        name="ffn_sample",
    )(x, st, st, st, st, w_up, w_up, w_fdw, w_fdw, b_fdw.reshape(1, 2 * f), b_fdw.reshape(1, 2 * f), w_down,
      g.reshape(1, d), b.reshape(1, d))
    new_state = jnp.stack([state[:, 1, :], jnp.concatenate([hg, hu], axis=1)], axis=1)
    return y, new_state


def kernel(x_prompt, x_sample, cache_k, cache_v, state_conv, state_ffn, page_table, w_qkv, w_o, w_pw1, b_pw1,
           w_dw, b_dw, ln_cv_g, ln_cv_b, w_pw2, b_pw2, w_up, w_fdw, b_fdw, w_down, ln1_g, ln1_b, ln2_g, ln2_b):
    bp, sp, d = x_prompt.shape
    bs, ss, _ = x_sample.shape
    assert d == D_MODEL and ss == 1
    slopes = jnp.asarray(_alibi_slopes())
    yp = x_prompt.reshape(bp * sp, d)
    ys = x_sample.reshape(bs, d)
    k_sa, v_sa, c_pr, c_sa, f_pr, f_sa = [], [], [], [], [], []
    kv_pr = None
    kv_shape = ((DEPTH + 1) // 2, bp, sp, N_HEADS, HEAD_DIM)
    for i in range(DEPTH):
        j = i // 2
        if i % 2 == 0:
            wq = w_qkv[j].astype(BF16)
            wo = w_o[j].astype(BF16)
            qa, k, v, kf, vf = _qkv_prompt(yp, wq, bp, sp, j, kv_pr, slopes)
            kv_pr = (kf, vf)
            op = _moba_prompt(qa, k, v, slopes)
            qkv_s = _mm(ys, wq)
            q_s, k_s, v_s = qkv_s[:, 0:d], qkv_s[:, d:2 * d], qkv_s[:, 2 * d:3 * d]
            os_ = _moba_sample(q_s, k_s, v_s, cache_k, cache_v, page_table, j, slopes)
            yp = _proj_ln(op, wo, None, yp, ln1_g[i], ln1_b[i])
            ys = _proj_ln(os_, wo, None, ys, ln1_g[i], ln1_b[i])
            k_sa.append(k_s.reshape(bs, ss, N_HEADS, HEAD_DIM))
            v_sa.append(v_s.reshape(bs, ss, N_HEADS, HEAD_DIM))
        else:
            w1 = w_pw1[j].astype(BF16)
            w2 = w_pw2[j].astype(BF16)
            up = _pw1_glu(yp, w1, b_pw1[j])
            zp = _dwconv_ln_prompt(up, w_dw[j], b_dw[j], ln_cv_g[j], ln_cv_b[j], sp)
            us = _pw1_glu(ys, w1, b_pw1[j])
            zs = _dwconv_ln_sample(state_conv[j], us, w_dw[j], b_dw[j], ln_cv_g[j], ln_cv_b[j])
            yp = _proj_ln(zp, w2, b_pw2[j], yp, ln1_g[i], ln1_b[i])
            ys = _proj_ln(zs, w2, b_pw2[j], ys, ln1_g[i], ln1_b[i])
            c_pr.append(up.reshape(bp, sp, d)[:, sp - (CONV_W - 1):, :])
            c_sa.append(jnp.concatenate([state_conv[j][:, 1:, :], us[:, None, :]], axis=1))
        wu = w_up[i].astype(BF16)
        wd = w_down[i].astype(BF16)
        yp, fp = _ffn_prompt(yp, wu, w_fdw[i], b_fdw[i], wd, ln2_g[i], ln2_b[i], bp, sp)
        ys, fs = _ffn_sample(ys, state_ffn[i], wu, w_fdw[i], b_fdw[i], wd, ln2_g[i], ln2_b[i])
        f_pr.append(fp)
        f_sa.append(fs)
    return (yp.reshape(bp, sp, d), ys.reshape(bs, ss, d), kv_pr[0].reshape(kv_shape), kv_pr[1].reshape(kv_shape),
            jnp.stack(k_sa),
            jnp.stack(v_sa), jnp.stack(c_pr), jnp.stack(c_sa), jnp.stack(f_pr), jnp.stack(f_sa))
```
